```python
import math
import jax
import jax.numpy as jnp
from jax import lax
import numpy as np

D_MODEL = 1024
BATCH = 32
SEQ = 2048
DEPTH = 1

GRID_W = 64
CTX_LEN = 256

M_HEADS = 8
M_QK_DIM = 64
M_V_DIM = 128
M_QK = M_HEADS * M_QK_DIM
M_V = M_HEADS * M_V_DIM
M_CHUNK = 64

H_HEADS = 8
H_K_DIM = 128
H_V_DIM = 128
H_K = H_HEADS * H_K_DIM
H_V = H_HEADS * H_V_DIM
H_CHUNK = 16

FFN_HIDDEN = 2816
N_MOD = 9
EPS = 1e-6

PROJ_SIZES = (M_QK, M_QK, M_V, M_V, 2 * M_HEADS, 2 * M_HEADS,
              H_K, 2 * H_K, H_V, H_V, D_MODEL, D_MODEL)

kernel_name = "hybrid_mlstm_hgrn2_macaron_dit"


def _rms(x, g):
    xf = x.astype(jnp.float32)
    y = xf * lax.rsqrt(jnp.mean(xf * xf, axis=-1, keepdims=True) + EPS)
    return y.astype(x.dtype) * g


def _head_rms(y, g, heads):
    B, T, W = y.shape
    return _rms(y.reshape(B, T, heads, W // heads), g.reshape(heads, W // heads)).reshape(B, T, W)


def _modulate(xn, shift, scale):
    return xn * (1.0 + scale) + shift


def _swiglu(xm, w_in, w_out):
    a, u = jnp.split(xm @ w_in, 2, axis=-1)
    return (jax.nn.silu(a) * u) @ w_out


def _split_proj(p):
    idx = np.cumsum(PROJ_SIZES)[:-1].tolist()
    return jnp.split(p, idx, axis=-1)


def _to_chunks(a, L):
    B, H, T = a.shape[:3]
    rest = a.shape[3:]
    return jnp.moveaxis(a.reshape(B, H, T // L, L, *rest), 2, 0)


def _from_chunks(a):
    NC, B, H, L = a.shape[:4]
    rest = a.shape[4:]
    return jnp.moveaxis(a, 0, 2).reshape(B, H, NC * L, *rest)


def _to_col_major(a):
    B, T, C = a.shape
    rows = T // GRID_W
    return a.reshape(B, rows, GRID_W, C).transpose(0, 2, 1, 3).reshape(B, T, C)


def _to_row_major(a):
    B, T, C = a.shape
    rows = T // GRID_W
    return a.reshape(B, GRID_W, rows, C).transpose(0, 2, 1, 3).reshape(B, T, C)


def _mlstm_scan(q, k, v, ig, fg, state):
    L = M_CHUNK
    mask = jnp.tril(jnp.ones((L, L), dtype=bool))
    scale = M_QK_DIM ** -0.5
    f32 = jnp.float32
    xs = (_to_chunks(q.astype(f32) * scale, L), _to_chunks(k.astype(f32), L),
          _to_chunks(v.astype(f32), L), _to_chunks(ig.astype(f32), L),
          _to_chunks(jax.nn.log_sigmoid(fg.astype(f32)), L))

    def body(carry, inp):
        C, n, m = carry
        qc, kc, vc, ic, lf = inp
        b = jnp.cumsum(lf, axis=-1)
        dlog = jnp.where(mask, b[..., :, None] - b[..., None, :] + ic[..., None, :], -jnp.inf)
        inter_log = b + m[..., None]
        m_t = jnp.maximum(inter_log, jnp.max(dlog, axis=-1))
        s = jnp.einsum('bhtd,bhsd->bhts', qc, kc) * jnp.exp(dlog - m_t[..., None])
        w_inter = jnp.exp(inter_log - m_t)
        num = w_inter[..., None] * jnp.einsum('bhtd,bhdv->bhtv', qc, C) + jnp.einsum('bhts,bhsv->bhtv', s, vc)
        den = w_inter * jnp.einsum('bhtd,bhd->bht', qc, n) + jnp.sum(s, axis=-1)
        h = num / jnp.maximum(jnp.abs(den), jnp.exp(-m_t))[..., None]
        b_last = b[..., -1]
        w_log = b_last[..., None] - b + ic
        m_new = jnp.maximum(b_last + m, jnp.max(w_log, axis=-1))
        decay = jnp.exp(b_last + m - m_new)
        w = jnp.exp(w_log - m_new[..., None])
        C_new = decay[..., None, None] * C + jnp.einsum('bhs,bhsd,bhsv->bhdv', w, kc, vc)
        n_new = decay[..., None] * n + jnp.einsum('bhs,bhsd->bhd', w, kc)
        return (C_new, n_new, m_new), h

    state, h = lax.scan(body, state, xs)
    return _from_chunks(h), state


def _hgrn_scan(q, k, v, logf, S):
    L = H_CHUNK
    mask = jnp.tril(jnp.ones((L, L), dtype=bool))[:, :, None]
    f32 = jnp.float32
    xs = tuple(_to_chunks(a.astype(f32), L) for a in (q, k, v, logf))

    def body(S, inp):
        qc, kc, vc, lf = inp
        b = jnp.cumsum(lf, axis=2)
        diff = b[:, :, :, None, :] - b[:, :, None, :, :]
        decay = jnp.exp(jnp.where(mask, diff, -jnp.inf))
        attn = jnp.einsum('bhtd,bhtsd,bhsd->bhts', qc, decay, kc)
        o = jnp.einsum('bhtd,bhdv->bhtv', qc * jnp.exp(b), S) + jnp.einsum('bhts,bhsv->bhtv', attn, vc)
        b_last = b[:, :, -1]
        S_new = jnp.exp(b_last)[..., None] * S + jnp.einsum(
            'bhsd,bhsv->bhdv', kc * jnp.exp(b_last[:, :, None, :] - b), vc)
        return S_new, o

    S, o = lax.scan(body, S, xs)
    return _from_chunks(o), S


def _flip_t(args):
    return tuple(jnp.flip(a, axis=2) for a in args)


def _bidir_scan(scan_fn, ctx_fwd, ctx_bwd, lat_fwd, lat_bwd, init):
    hcf, s_f = scan_fn(*ctx_fwd, init)
    hlf, _ = scan_fn(*lat_fwd, s_f)
    hcb, s_b = scan_fn(*_flip_t(ctx_bwd), init)
    hlb, _ = scan_fn(*_flip_t(lat_bwd), s_b)
    return hcf + jnp.flip(hcb, axis=2), hlf + jnp.flip(hlb, axis=2)


def _mlstm_branch(seg_c, seg_l, gain, dtype):
    f32 = jnp.float32

    def prep(seg):
        q, k, v, o, ig, fg = seg
        B, T, _ = q.shape
        hd = lambda a: a.reshape(B, T, M_HEADS, -1).transpose(0, 2, 1, 3)
        ig = ig.astype(f32).transpose(0, 2, 1)
        fg = fg.astype(f32).transpose(0, 2, 1)
        qh, kh, vh = hd(q), hd(k), hd(v)
        fwd = (qh, kh, vh, ig[:, :M_HEADS], fg[:, :M_HEADS])
        bwd = (qh, kh, vh, ig[:, M_HEADS:], fg[:, M_HEADS:])
        return fwd, bwd, o

    cf, cb, oc = prep(seg_c)
    lf, lb, ol = prep(seg_l)
    B = oc.shape[0]
    init = (jnp.zeros((B, M_HEADS, M_QK_DIM, M_V_DIM), f32),
            jnp.zeros((B, M_HEADS, M_QK_DIM), f32),
            jnp.zeros((B, M_HEADS), f32))
    hc, hl = _bidir_scan(_mlstm_scan, cf, cb, lf, lb, init)

    def out(h, o):
        B_, H, T, dv = h.shape
        h = h.transpose(0, 2, 1, 3).reshape(B_, T, H * dv)
        return (_head_rms(h, gain, M_HEADS) * jax.nn.sigmoid(o.astype(f32))).astype(dtype)

    return out(hc, oc), out(hl, ol)


def _hgrn_gates(f_pre, lb):
    fp = f_pre.astype(jnp.float32)
    logf = jnp.log(lb + (1.0 - lb) * jax.nn.sigmoid(fp))
    k = (1.0 - lb) * jax.nn.sigmoid(-fp)
    return logf, k


def _hgrn_branch(seg_c, seg_l, lb, gain, dtype):
    f32 = jnp.float32

    def prep(seg):
        q, f, i, g = seg
        B, T, _ = q.shape
        hd = lambda a: a.astype(f32).reshape(B, T, H_HEADS, -1).transpose(0, 2, 1, 3)
        qh = hd(jax.nn.silu(q.astype(f32)))
        vh = hd(i)
        lf_f, k_f = _hgrn_gates(f[..., :H_K], lb[0])
        lf_b, k_b = _hgrn_gates(f[..., H_K:], lb[1])
        return (qh, hd(k_f), vh, hd(lf_f)), (qh, hd(k_b), vh, hd(lf_b)), g

    cf, cb, gc = prep(seg_c)
    lf, lbw, gl = prep(tuple(_to_col_major(a) for a in seg_l))
    B = gc.shape[0]
    init = jnp.zeros((B, H_HEADS, H_K_DIM, H_V_DIM), f32)
    hc, hl = _bidir_scan(_hgrn_scan, cf, cb, lf, lbw, init)

    def out(h, g):
        B_, H, T, dv = h.shape
        h = h.transpose(0, 2, 1, 3).reshape(B_, T, H * dv)
        return (_head_rms(h, gain, H_HEADS) * jax.nn.silu(g.astype(f32))).astype(dtype)

    return out(hc, gc), _to_row_major(out(hl, gl))


def _token_mix(xc, xl, w_in, b_in, m_gain, lb, h_gain, proj_m, proj_h, w_out, need_ctx):
    seg_c = _split_proj(xc @ w_in + b_in)
    seg_l = _split_proj(xl @ w_in + b_in)
    hm_c, hm_l = _mlstm_branch(seg_c[0:6], seg_l[0:6], m_gain, xl.dtype)
    hh_c, hh_l = _hgrn_branch(seg_c[6:10], seg_l[6:10], lb, h_gain, xl.dtype)

    def merge(hm, hh, gm, gh):
        return (jax.nn.sigmoid(gm) * (hm @ proj_m) + jax.nn.sigmoid(gh) * (hh @ proj_h)) @ w_out

    y_l = merge(hm_l, hh_l, seg_l[10], seg_l[11])
    y_c = merge(hm_c, hh_c, seg_c[10], seg_c[11]) if need_ctx else None
    return y_c, y_l


def setup_inputs(seed: int = 0) -> dict:
    key = jax.random.key(seed)
    ks = jax.random.split(key, 24)
    D, F = D_MODEL, FFN_HIDDEN
    P = int(sum(PROJ_SIZES))
    nrm = lambda k, shape, s: jax.random.normal(k, shape, jnp.float32) * s
    gains = lambda k, shape: 1.0 + nrm(k, shape, 0.05)
    f_start = int(sum(PROJ_SIZES[:5]))
    f_bias = jnp.tile(jnp.linspace(3.0, 6.0, M_HEADS), 2)
    b_off = jnp.zeros((P,), jnp.float32).at[f_start:f_start + 2 * M_HEADS].set(f_bias)
    return {
        "x": nrm(ks[0], (BATCH, SEQ, D), 1.0),
        "c": nrm(ks[1], (BATCH, D), 1.0),
        "ctx": nrm(ks[2], (BATCH, CTX_LEN, D), 1.0),
        "c_ctx": nrm(ks[3], (D,), 1.0),
        "ada_w": nrm(ks[4], (DEPTH, D, N_MOD * D), D ** -0.5),
        "ada_b": nrm(ks[5], (DEPTH, N_MOD * D), 0.02),
        "ffn1_norm": gains(ks[6], (DEPTH, D)),
        "ffn1_w_in": nrm(ks[7], (DEPTH, D, 2 * F), D ** -0.5),
        "ffn1_w_out": nrm(ks[8], (DEPTH, F, D), F ** -0.5),
        "mix_norm": gains(ks[9], (DEPTH, D)),
        "mix_w_in": nrm(ks[10], (DEPTH, D, P), D ** -0.5),
        "mix_b_in": b_off[None, :] + nrm(ks[11], (DEPTH, P), 0.02),
        "mlstm_norm": gains(ks[12], (DEPTH, M_V)),
        "hgrn_lb_logits": nrm(ks[13], (2, DEPTH + 1, H_K), 0.5),
        "hgrn_norm": gains(ks[14], (DEPTH, H_V)),
        "proj_m": nrm(ks[15], (DEPTH, M_V, D), M_V ** -0.5),
        "proj_h": nrm(ks[16], (DEPTH, H_V, D), H_V ** -0.5),
        "mix_w_out": nrm(ks[17], (DEPTH, D, D), D ** -0.5),
        "ffn2_norm": gains(ks[18], (DEPTH, D)),
        "ffn2_w_in": nrm(ks[19], (DEPTH, D, 2 * F), D ** -0.5),
        "ffn2_w_out": nrm(ks[20], (DEPTH, F, D), F ** -0.5),
        "final_norm": gains(ks[21], (D,)),
    }


def reference(x, c, ctx, c_ctx, ada_w, ada_b, ffn1_norm, ffn1_w_in, ffn1_w_out, mix_norm,
              mix_w_in, mix_b_in, mlstm_norm, hgrn_lb_logits, hgrn_norm, proj_m, proj_h,
              mix_w_out, ffn2_norm, ffn2_w_in, ffn2_w_out, final_norm):
    B = x.shape[0]
    D = x.shape[-1]
    lb_all = jnp.cumsum(jax.nn.softmax(hgrn_lb_logits.astype(jnp.float32), axis=1), axis=1)
    h = ctx
    for l in range(DEPTH):
        last = l == DEPTH - 1
        ml = (jax.nn.silu(c) @ ada_w[l] + ada_b[l]).reshape(B, N_MOD, 1, D)
        mc = (jax.nn.silu(c_ctx) @ ada_w[l] + ada_b[l]).reshape(N_MOD, 1, 1, D)
        x = x + 0.5 * ml[:, 2] * _swiglu(_modulate(_rms(x, ffn1_norm[l]), ml[:, 0], ml[:, 1]), ffn1_w_in[l], ffn1_w_out[l])
        h = h + 0.5 * mc[2] * _swiglu(_modulate(_rms(h, ffn1_norm[l]), mc[0], mc[1]), ffn1_w_in[l], ffn1_w_out[l])
        y_c, y_l = _token_mix(_modulate(_rms(h, mix_norm[l]), mc[3], mc[4]),
                              _modulate(_rms(x, mix_norm[l]), ml[:, 3], ml[:, 4]),
                              mix_w_in[l], mix_b_in[l], mlstm_norm[l], lb_all[:, l], hgrn_norm[l],
                              proj_m[l], proj_h[l], mix_w_out[l], not last)
        x = x + ml[:, 5] * y_l
        if not last:
            h = h + mc[5] * y_c
            h = h + 0.5 * mc[8] * _swiglu(_modulate(_rms(h, ffn2_norm[l]), mc[6], mc[7]), ffn2_w_in[l], ffn2_w_out[l])
        x = x + 0.5 * ml[:, 8] * _swiglu(_modulate(_rms(x, ffn2_norm[l]), ml[:, 6], ml[:, 7]), ffn2_w_in[l], ffn2_w_out[l])
    return _rms(x, final_norm)
```

```python
import functools

import jax
import jax.numpy as jnp
from jax import lax
from jax.experimental import pallas as pl
from jax.experimental.pallas import tpu as pltpu

F32 = jnp.float32
BF16 = jnp.bfloat16
EPS = 1e-6

GRID_W = 64
M_HEADS = 8
M_QK_DIM = 64
M_V_DIM = 128
H_HEADS = 8
H_DIM = 128
N_MOD = 9

LANES = 128
TOKEN_TILE = 512
FFN_CHUNK = 256
PROJ_CHUNK = 512
M_CHUNK = 128
H_CHUNK = 64
HEADS_PER_STEP = 4
VMEM_LIMIT = 56 * 1024 * 1024

_NT = (((1,), (1,)), ((), ()))


def _params(sem, vmem=VMEM_LIMIT):
    return pltpu.CompilerParams(dimension_semantics=sem, vmem_limit_bytes=vmem)


def _resident(shape):
    nd = len(shape)
    return pl.BlockSpec(shape, lambda *_: (0,) * nd, pipeline_mode=pl.Buffered(1))


def _dot(a, b):
    return jnp.dot(a, b, preferred_element_type=F32)


def _rms(x, g):
    ms = jnp.mean(x * x, axis=-1, keepdims=True)
    return x * lax.rsqrt(ms + EPS) * g


def _split3(x):
    hi = x.astype(BF16)
    r = x - hi.astype(F32)
    mid = r.astype(BF16)
    lo = (r - mid.astype(F32)).astype(BF16)
    return hi, mid, lo


def _tri_cumsum(tri, x):
    hi, mid, lo = _split3(x)
    return _dot(tri, hi) + _dot(tri, mid) + _dot(tri, lo)


def _mod_kernel(c_ref, w_ref, b_ref, o_ref):
    c = c_ref[...]
    a = c * jax.nn.sigmoid(c)
    a_hi = a.astype(BF16)
    a_lo = (a - a_hi.astype(F32)).astype(BF16)
    w = w_ref[...]
    w_hi = w.astype(BF16)
    w_lo = (w - w_hi.astype(F32)).astype(BF16)
    o_ref[...] = _dot(a_hi, w_hi) + _dot(a_hi, w_lo) + _dot(a_lo, w_hi) + b_ref[...]


def _modulation(cc, w, b):
    R, D = cc.shape
    N = w.shape[1]
    tn = 1024
    return pl.pallas_call(
        _mod_kernel,
        out_shape=jax.ShapeDtypeStruct((R, N), F32),
        grid=(N // tn,),
        in_specs=[pl.BlockSpec((R, D), lambda j: (0, 0)),
                  pl.BlockSpec((D, tn), lambda j: (0, j)),
                  pl.BlockSpec((1, tn), lambda j: (0, j))],
        out_specs=pl.BlockSpec((R, tn), lambda j: (0, j)),
        compiler_params=_params(("arbitrary",)),
        name="modulation",
    )(cc, w, b)


def _ffn_kernel(x_ref, mod_ref, g_ref, win_ref, wout_ref, g2_ref, *outs,
                rows, rows2, out_x, aux, n_chunks):
    fc = FFN_CHUNK
    x = x_ref[...]
    mod = mod_ref[0]
    shift = mod[rows[0]:rows[0] + 1]
    scale = mod[rows[1]:rows[1] + 1]
    gate = mod[rows[2]:rows[2] + 1]
    xm = (_rms(x, g_ref[...]) * (1.0 + scale) + shift).astype(BF16)
    acc = jnp.zeros(x.shape, F32)
    for j in range(n_chunks):
        au = _dot(xm, win_ref[:, j * 2 * fc:(j + 1) * 2 * fc])
        a = au[:, :fc]
        u = au[:, fc:]
        h = (a * jax.nn.sigmoid(a) * u).astype(BF16)
        acc = acc + _dot(h, wout_ref[j * fc:(j + 1) * fc, :])
    xn = x + 0.5 * gate * acc
    k = 0
    if out_x:
        outs[k][...] = xn
        k += 1
    if aux == "mix":
        shift2 = mod[rows2[0]:rows2[0] + 1]
        scale2 = mod[rows2[1]:rows2[1] + 1]
        outs[k][...] = (_rms(xn, g2_ref[...]) * (1.0 + scale2) + shift2).astype(BF16)
    elif aux == "final":
        outs[k][...] = _rms(xn, g2_ref[...])


def _ffn(x2d, mod, tiles_per_mod, g, win, wout, g2, *, rows, rows2, out_x, aux):
    N, D = x2d.shape
    F = wout.shape[0]
    tm = TOKEN_TILE
    out_shape, out_specs = [], []
    if out_x:
        out_shape.append(jax.ShapeDtypeStruct((N, D), F32))
        out_specs.append(pl.BlockSpec((tm, D), lambda i: (i, 0)))
    if aux == "mix":
        out_shape.append(jax.ShapeDtypeStruct((N, D), BF16))
        out_specs.append(pl.BlockSpec((tm, D), lambda i: (i, 0)))
    elif aux == "final":
        out_shape.append(jax.ShapeDtypeStruct((N, D), F32))
        out_specs.append(pl.BlockSpec((tm, D), lambda i: (i, 0)))
    if tiles_per_mod is None:
        mod_map = lambda i: (0, 0, 0)
    else:
        mod_map = lambda i: (i // tiles_per_mod, 0, 0)
    kern = functools.partial(_ffn_kernel, rows=rows, rows2=rows2, out_x=out_x, aux=aux,
                             n_chunks=F // FFN_CHUNK)
    return pl.pallas_call(
        kern,
        out_shape=out_shape,
        grid=(N // tm,),
        in_specs=[pl.BlockSpec((tm, D), lambda i: (i, 0)),
                  pl.BlockSpec((1, N_MOD, D), mod_map),
                  _resident((1, D)),
                  _resident(win.shape),
                  _resident(wout.shape),
                  _resident((1, D))],
        out_specs=out_specs,
        compiler_params=_params(("parallel",)),
        name="ffn_" + aux,
    )(x2d, mod, g, win, wout, g2)


def _proj_kernel(x_ref, w_ref, b_ref, lb_ref, *outs, segs):
    x = x_ref[...]
    c0 = 0
    oi = 0
    for wd, kind, arg in segs:
        for s in range(0, wd, PROJ_CHUNK):
            w = min(PROJ_CHUNK, wd - s)
            y = _dot(x, w_ref[:, c0 + s:c0 + s + w]) + b_ref[:, c0 + s:c0 + s + w]
            if kind == "bf16":
                outs[oi][:, s:s + w] = y.astype(BF16)
            elif kind == "f32":
                outs[oi][:, s:s + w] = y
            elif kind == "sigmoid":
                outs[oi][:, s:s + w] = jax.nn.sigmoid(y).astype(BF16)
            elif kind == "silu":
                outs[oi][:, s:s + w] = (y * jax.nn.sigmoid(y)).astype(BF16)
            elif kind == "hgrn":
                lb = lb_ref[arg:arg + 1, s:s + w]
                f = lb + (1.0 - lb) * jax.nn.sigmoid(y)
                outs[oi][:, s:s + w] = jnp.log(f)
                outs[oi + 1][:, s:s + w] = ((1.0 - lb) * jax.nn.sigmoid(-y)).astype(BF16)
        c0 += wd
        oi += 2 if kind == "hgrn" else 1


def _proj(x2d, w, b, lb, segs, name):
    N, D = x2d.shape
    tm = TOKEN_TILE
    out_shape, out_specs = [], []
    for wd, kind, _ in segs:
        dts = {"bf16": [BF16], "f32": [F32], "sigmoid": [BF16], "silu": [BF16],
               "hgrn": [F32, BF16]}[kind]
        for dt in dts:
            out_shape.append(jax.ShapeDtypeStruct((N, wd), dt))
            out_specs.append(pl.BlockSpec((tm, wd), lambda i: (i, 0)))
    return pl.pallas_call(
        functools.partial(_proj_kernel, segs=tuple(segs)),
        out_shape=out_shape,
        grid=(N // tm,),
        in_specs=[pl.BlockSpec((tm, D), lambda i: (i, 0)),
                  _resident(w.shape),
                  _resident(b.shape),
                  _resident(lb.shape)],
        out_specs=out_specs,
        compiler_params=_params(("parallel",)),
        name=name,
    )(x2d, w, b, lb)


def _mlstm_kernel(q_ref, k_ref, v_ref, g_ref, so_ref, kc_ref, vc_ref, gc_ref, gain_ref,
                  out_ref, hacc_ref, c_ref, m_ref):
    L = M_CHUNK
    hps = HEADS_PER_STEP
    T = q_ref.shape[0]
    Tc = kc_ref.shape[0]
    nt, nct = T // L, Tc // L
    row = lax.broadcasted_iota(jnp.int32, (L, L), 0)
    col = lax.broadcasted_iota(jnp.int32, (L, L), 1)
    lane = lax.broadcasted_iota(jnp.int32, (L, LANES), 1)
    krow = lax.broadcasted_iota(jnp.int32, (LANES, 1), 0)
    ones_ext = (lane == 0).astype(BF16)
    fg_lane = (lane >= 2 * hps) & (lane < 4 * hps)

    for d in (0, 1):
        tri = (col <= row) if d == 0 else (col >= row)
        trib = tri.astype(BF16)
        c_ref[...] = jnp.zeros(c_ref.shape, F32)
        m_ref[...] = jnp.zeros(m_ref.shape, F32)

        def chunk(r0, gref, kref, vref, want_out, d=d, tri=tri, trib=trib):
            G = gref[pl.ds(r0, L), :]
            B = _tri_cumsum(trib, jax.nn.log_sigmoid(G))
            XT = jnp.where(fg_lane, B, G).T
            blast_row = B[L - 1:L, :] if d == 0 else B[0:1, :]
            for p in range(hps // 2):
                k2 = kref[pl.ds(r0, L), p * LANES:(p + 1) * LANES]
                k2f = k2.astype(F32)
                C2 = c_ref[p]
                if want_out:
                    q2 = q_ref[pl.ds(r0, L), p * LANES:(p + 1) * LANES]
                    C2b = C2.astype(BF16)
                upd = None
                decay_rows = None
                for a in range(2):
                    i = 2 * p + a
                    ji = hps * d + i
                    jf = 2 * hps + hps * d + i
                    cs = slice(i * LANES, (i + 1) * LANES)
                    vext = jnp.concatenate([vref[pl.ds(r0, L), cs], ones_ext], axis=1)
                    bcol = B[:, jf:jf + 1]
                    icol = G[:, ji:ji + 1]
                    b_last = blast_row[:, jf:jf + 1]
                    m_prev = m_ref[i:i + 1, 0:1]
                    headmask = (lane < M_QK_DIM) if a == 0 else (lane >= M_QK_DIM)
                    if want_out:
                        brow = XT[jf:jf + 1, :]
                        irow = XT[ji:ji + 1, :]
                        qa = jnp.where(headmask, q2, jnp.zeros_like(q2))
                        s_raw = lax.dot_general(qa, k2, _NT, preferred_element_type=F32)
                        dlog = jnp.where(tri, bcol - brow + irow, -jnp.inf)
                        inter = bcol + m_prev
                        m_t = jnp.maximum(inter, jnp.max(dlog, axis=1, keepdims=True))
                        s = s_raw * jnp.exp(dlog - m_t)
                        nd = (jnp.exp(inter - m_t) * _dot(qa, C2b)
                              + _dot(s.astype(BF16), vext))
                        den = jnp.maximum(jnp.abs(nd[:, LANES:LANES + 1]), jnp.exp(-m_t))
                        h = nd[:, :LANES] / den
                        if d == 0:
                            hacc_ref[pl.ds(r0, L), cs] = h
                        else:
                            hs = hacc_ref[pl.ds(r0, L), cs] + h
                            y = (_rms(hs, gain_ref[:, cs])
                                 * so_ref[pl.ds(r0, L), cs].astype(F32))
                            out_ref[pl.ds(r0, L), cs] = y.astype(BF16)
                    wlog = b_last - bcol + icol
                    m_new = jnp.maximum(b_last + m_prev, jnp.max(wlog, axis=0, keepdims=True))
                    decay = jnp.exp(b_last + m_prev - m_new)
                    kw = jnp.where(headmask, k2f * jnp.exp(wlog - m_new), 0.0)
                    u = _dot(kw.T.astype(BF16), vext)
                    upd = u if upd is None else upd + u
                    rowmask = (krow < M_QK_DIM) if a == 0 else (krow >= M_QK_DIM)
                    dr = jnp.where(rowmask, decay, 0.0)
                    decay_rows = dr if decay_rows is None else decay_rows + dr
                    m_ref[i:i + 1, :] = jnp.broadcast_to(m_new, (1, LANES))
                c_ref[p] = decay_rows * C2 + upd

        def ctx_body(n, carry, d=d, chunk=chunk):
            c = n if d == 0 else nct - 1 - n
            chunk(pl.multiple_of(c * L, L), gc_ref, kc_ref, vc_ref, False)
            return carry

        def lat_body(n, carry, d=d, chunk=chunk):
            c = n if d == 0 else nt - 1 - n
            chunk(pl.multiple_of(c * L, L), g_ref, k_ref, v_ref, True)
            return carry

        lax.fori_loop(0, nct, ctx_body, 0)
        lax.fori_loop(0, nt, lat_body, 0)


def _mlstm(q, k, v, g, so, kc, vc, gc, gain):
    B, T, _ = q.shape
    Tc = kc.shape[1]
    hps = HEADS_PER_STEP
    G = M_HEADS // hps
    wq, wv = hps * M_QK_DIM, hps * M_V_DIM
    lat = lambda w: pl.BlockSpec((None, T, w), lambda b, g: (b, 0, g))
    cx = lambda w: pl.BlockSpec((None, Tc, w), lambda b, g: (b, 0, g))
    return pl.pallas_call(
        _mlstm_kernel,
        out_shape=jax.ShapeDtypeStruct((B, T, M_HEADS * M_V_DIM), BF16),
        grid=(B, G),
        in_specs=[lat(wq), lat(wq), lat(wv), lat(LANES), lat(wv),
                  cx(wq), cx(wv), cx(LANES),
                  pl.BlockSpec((1, wv), lambda b, g: (0, g))],
        out_specs=lat(wv),
        scratch_shapes=[pltpu.VMEM((T, wv), F32),
                        pltpu.VMEM((hps // 2, 2 * M_QK_DIM, 2 * LANES), F32),
                        pltpu.VMEM((8, LANES), F32)],
        compiler_params=_params(("parallel", "parallel")),
        name="mlstm_scan",
    )(q, k, v, g, so, kc, vc, gc, gain)


def _pivot_rows(b, h, off):
    L, N = b.shape
    bc = lambda r, n: jnp.broadcast_to(b[r:r + 1, :], (n, N))
    if 2 * h >= 8:
        return jnp.concatenate([bc(x + off, 2 * h) for x in range(0, L, 2 * h)], axis=0)
    sub = lax.broadcasted_iota(jnp.int32, (8, N), 0)
    tiles = []
    for x in range(0, L, 8):
        t = bc(x + 8 - 2 * h + off, 8)
        for y in range(8 - 4 * h, -1, -2 * h):
            t = jnp.where(sub < y + 2 * h, bc(x + y + off, 8), t)
        tiles.append(t)
    return jnp.concatenate(tiles, axis=0)


def _hgrn_kernel(q_ref, kf_ref, kb_ref, v_ref, lff_ref, lfb_ref, sg_ref,
                 kfc_ref, kbc_ref, vc_ref, lffc_ref, lfbc_ref, gain_ref,
                 out_ref, hacc_ref, st_ref):
    L = H_CHUNK
    hps = HEADS_PER_STEP
    T = q_ref.shape[0]
    Tc = vc_ref.shape[0]
    nt, nct = T // L, Tc // L
    row = lax.broadcasted_iota(jnp.int32, (L, L), 0)
    col = lax.broadcasted_iota(jnp.int32, (L, L), 1)
    tok = lax.broadcasted_iota(jnp.int32, (L, H_DIM), 0)
    halves = [L >> (n + 1) for n in range(L.bit_length() - 1)]

    for d in (0, 1):
        tri = (col <= row) if d == 0 else (col >= row)
        trib = tri.astype(BF16)
        st_ref[...] = jnp.zeros(st_ref.shape, F32)
        k_lat, lf_lat = (kf_ref, lff_ref) if d == 0 else (kb_ref, lfb_ref)
        k_ctx, lf_ctx = (kfc_ref, lffc_ref) if d == 0 else (kbc_ref, lfbc_ref)
        q_side, k_side, pair = [], [], []
        for h in halves:
            late = (tok & (2 * h - 1)) >= h
            q_side.append(late if d == 0 else ~late)
            k_side.append(~late if d == 0 else late)
            same = (row & -(2 * h)) == (col & -(2 * h))
            r_late = (row & (2 * h - 1)) >= h
            c_late = (col & (2 * h - 1)) >= h
            pair.append(same & (r_late & ~c_late if d == 0 else ~r_late & c_late))

        def chunk(r0, kref, lfref, vref, want_out, d=d, trib=trib,
                  q_side=q_side, k_side=k_side, pair=pair):
            rows = pl.ds(r0, L)
            b = _tri_cumsum(trib, lfref[rows, :])
            blast = b[L - 1:L, :] if d == 0 else b[0:1, :]
            for i in range(hps):
                cs = slice(i * H_DIM, (i + 1) * H_DIM)
                b_h = b[:, cs]
                bl = blast[:, cs]
                k_h = kref[rows, cs].astype(F32)
                v_h = vref[rows, cs]
                ST = st_ref[i]
                if want_out:
                    q_h = q_ref[rows, cs].astype(F32)
                    attn = jnp.zeros((L, L), F32)
                    for n, h in enumerate(halves):
                        z = jnp.exp(-jnp.abs(b_h - _pivot_rows(b_h, h, h - 1 + d)))
                        qs = jnp.where(q_side[n], q_h * z, 0.0).astype(BF16)
                        ks = jnp.where(k_side[n], k_h * z, 0.0).astype(BF16)
                        qk = lax.dot_general(qs, ks, _NT, preferred_element_type=F32)
                        attn = attn + jnp.where(pair[n], qk, 0.0)
                    qc = (q_h * jnp.exp(b_h)).astype(BF16)
                    o = (lax.dot_general(qc, ST.astype(BF16), _NT, preferred_element_type=F32)
                         + _dot(attn.astype(BF16), v_h)
                         + jnp.sum(q_h * k_h, axis=1, keepdims=True) * v_h.astype(F32))
                    if d == 0:
                        hacc_ref[rows, cs] = o
                    else:
                        hs = hacc_ref[rows, cs] + o
                        y = _rms(hs, gain_ref[:, cs]) * sg_ref[rows, cs].astype(F32)
                        out_ref[rows, cs] = y.astype(BF16)
                kd = (k_h * jnp.exp(bl - b_h)).astype(BF16)
                vT = v_h.astype(F32).T.astype(BF16)
                st_ref[i] = ST * jnp.exp(bl) + _dot(vT, kd)

        def ctx_body(n, carry, d=d, chunk=chunk, k_ctx=k_ctx, lf_ctx=lf_ctx):
            c = n if d == 0 else nct - 1 - n
            chunk(pl.multiple_of(c * L, L), k_ctx, lf_ctx, vc_ref, False)
            return carry

        def lat_body(n, carry, d=d, chunk=chunk, k_lat=k_lat, lf_lat=lf_lat):
            c = n if d == 0 else nt - 1 - n
            chunk(pl.multiple_of(c * L, L), k_lat, lf_lat, v_ref, True)
            return carry

        lax.fori_loop(0, nct, ctx_body, 0)
        lax.fori_loop(0, nt, lat_body, 0)


def _hgrn(q, kf, kb, v, lff, lfb, sg, kfc, kbc, vc, lffc, lfbc, gain):
    B, T, W = q.shape
    Tc = vc.shape[1]
    hps = HEADS_PER_STEP
    G = H_HEADS // hps
    w = hps * H_DIM
    lat = pl.BlockSpec((None, T, w), lambda b, g: (b, 0, g))
    cx = pl.BlockSpec((None, Tc, w), lambda b, g: (b, 0, g))
    return pl.pallas_call(
        _hgrn_kernel,
        out_shape=jax.ShapeDtypeStruct((B, T, W), BF16),
        grid=(B, G),
        in_specs=[lat] * 7 + [cx] * 5 + [pl.BlockSpec((1, w), lambda b, g: (0, g))],
        out_specs=lat,
        scratch_shapes=[pltpu.VMEM((T, w), F32),
                        pltpu.VMEM((hps, H_DIM, H_DIM), F32)],
        compiler_params=_params(("parallel", "parallel")),
        name="hgrn_scan",
    )(q, kf, kb, v, lff, lfb, sg, kfc, kbc, vc, lffc, lfbc, gain)


def _merge_kernel(hm_ref, hh_ref, gm_ref, gh_ref, x_ref, mod_ref, pm_ref, ph_ref, wo_ref,
                  out_ref, *, gate_row):
    a = _dot(hm_ref[...], pm_ref[...])
    b = _dot(hh_ref[...], ph_ref[...])
    z = (gm_ref[...].astype(F32) * a + gh_ref[...].astype(F32) * b).astype(BF16)
    y = _dot(z, wo_ref[...])
    gate = mod_ref[0][gate_row:gate_row + 1]
    out_ref[...] = x_ref[...] + gate * y


def _merge(hm, hh, gm, gh, x2d, mod, tiles_per_mod, pm, ph, wo):
    N, D = x2d.shape
    tm = TOKEN_TILE
    tile = pl.BlockSpec((tm, D), lambda i: (i, 0))
    return pl.pallas_call(
        functools.partial(_merge_kernel, gate_row=5),
        out_shape=jax.ShapeDtypeStruct((N, D), F32),
        grid=(N // tm,),
        in_specs=[tile, tile, tile, tile, tile,
                  pl.BlockSpec((1, N_MOD, D), lambda i: (i // tiles_per_mod, 0, 0)),
                  _resident(pm.shape), _resident(ph.shape), _resident(wo.shape)],
        out_specs=tile,
        compiler_params=_params(("parallel",)),
        name="merge",
    )(hm, hh, gm, gh, x2d, mod, pm, ph, wo)


def _interleave_ffn_in(w):
    D, F2 = w.shape
    F = F2 // 2
    a = w[:, :F].reshape(D, F // FFN_CHUNK, 1, FFN_CHUNK)
    u = w[:, F:].reshape(D, F // FFN_CHUNK, 1, FFN_CHUNK)
    return jnp.concatenate([a, u], axis=2).reshape(D, F2)


def _to_col_major(a):
    B, T, C = a.shape
    return a.reshape(B, T // GRID_W, GRID_W, C).transpose(0, 2, 1, 3).reshape(B, T, C)


def _to_row_major(a):
    B, T, C = a.shape
    return a.reshape(B, GRID_W, T // GRID_W, C).transpose(0, 2, 1, 3).reshape(B, T, C)


def kernel(x, c, ctx, c_ctx, ada_w, ada_b, ffn1_norm, ffn1_w_in, ffn1_w_out, mix_norm,
           mix_w_in, mix_b_in, mlstm_norm, hgrn_lb_logits, hgrn_norm, proj_m, proj_h,
           mix_w_out, ffn2_norm, ffn2_w_in, ffn2_w_out, final_norm):
    B, T, D = x.shape
    Tc = ctx.shape[1]
    hps = HEADS_PER_STEP
    MQ, MV, HW = M_HEADS * M_QK_DIM, M_HEADS * M_V_DIM, H_HEADS * H_DIM

    pad = (-(B + 1)) % 8
    cc = jnp.concatenate([c, c_ctx[None, :], jnp.zeros((pad, D), F32)], axis=0)
    mod = _modulation(cc, ada_w[0], ada_b[0][None, :])
    ml = mod[:B].reshape(B, N_MOD, D)
    mc = mod[B:B + 1].reshape(1, N_MOD, D)

    row = lambda v: v.reshape(1, -1).astype(F32)
    w1_in = _interleave_ffn_in(ffn1_w_in[0]).astype(BF16)
    w1_out = ffn1_w_out[0].astype(BF16)
    w2_in = _interleave_ffn_in(ffn2_w_in[0]).astype(BF16)
    w2_out = ffn2_w_out[0].astype(BF16)

    W, bias = mix_w_in[0], mix_b_in[0]
    o = 0
    seg = {}
    for name, wd in (("mq", MQ), ("mk", MQ), ("mv", MV), ("mo", MV), ("ig", 2 * M_HEADS),
                     ("fg", 2 * M_HEADS), ("hq", HW), ("hff", HW), ("hfb", HW), ("hi", HW),
                     ("hg", HW), ("gm", D), ("gh", D)):
        seg[name] = (o, o + wd)
        o += wd
    cols = lambda name: jnp.arange(*seg[name])
    gate_cols = []
    for g in range(M_HEADS // hps):
        heads = jnp.arange(g * hps, (g + 1) * hps)
        gate_cols.append(jnp.concatenate([
            seg["ig"][0] + heads, seg["ig"][0] + M_HEADS + heads,
            seg["fg"][0] + heads, seg["fg"][0] + M_HEADS + heads]))

    def gather(parts, scale_q=False):
        ws, bs = [], []
        for p in parts:
            if p == "gates":
                for gc in gate_cols:
                    ws += [W[:, gc], jnp.zeros((D, LANES - 4 * hps), F32)]
                    bs += [bias[gc], jnp.zeros((LANES - 4 * hps,), F32)]
            else:
                s = M_QK_DIM ** -0.5 if p == "mq" else 1.0
                ws.append(W[:, seg[p][0]:seg[p][1]] * s)
                bs.append(bias[seg[p][0]:seg[p][1]] * s)
        return jnp.concatenate(ws, axis=1).astype(BF16), jnp.concatenate(bs)[None, :]

    n_gate = LANES * (M_HEADS // hps)
    lb = jnp.cumsum(jax.nn.softmax(hgrn_lb_logits.astype(F32), axis=1), axis=1)[:, 0]

    x2d = x.reshape(B * T, D)
    c2d = ctx.reshape(B * Tc, D)
    tpb = T // TOKEN_TILE
    (xm_c,) = _ffn(c2d, mc, None, row(ffn1_norm[0]), w1_in, w1_out, row(mix_norm[0]),
                   rows=(0, 1, 2), rows2=(3, 4), out_x=False, aux="mix")
    x1, xm_l = _ffn(x2d, ml, tpb, row(ffn1_norm[0]), w1_in, w1_out, row(mix_norm[0]),
                    rows=(0, 1, 2), rows2=(3, 4), out_x=True, aux="mix")

    wm_l, bm_l = gather(["mq", "mk", "mv", "mo", "gates", "gm", "gh"])
    mq, mk, mv, mso, mg, sgm, sgh = _proj(
        xm_l, wm_l, bm_l, lb,
        [(MQ, "bf16", 0), (MQ, "bf16", 0), (MV, "bf16", 0), (MV, "sigmoid", 0),
         (n_gate, "f32", 0), (D, "sigmoid", 0), (D, "sigmoid", 0)], "proj_mlstm")
    wm_c, bm_c = gather(["mk", "mv", "gates"])
    mkc, mvc, mgc = _proj(xm_c, wm_c, bm_c, lb,
                          [(MQ, "bf16", 0), (MV, "bf16", 0), (n_gate, "f32", 0)], "proj_mlstm_ctx")

    xm_cm = _to_col_major(xm_l.reshape(B, T, D)).reshape(B * T, D)
    wh_l, bh_l = gather(["hq", "hff", "hfb", "hi", "hg"])
    hq, lff, hkf, lfb, hkb, hv, hsg = _proj(
        xm_cm, wh_l, bh_l, lb,
        [(HW, "silu", 0), (HW, "hgrn", 0), (HW, "hgrn", 1), (HW, "bf16", 0), (HW, "silu", 0)],
        "proj_hgrn")
    wh_c, bh_c = gather(["hff", "hfb", "hi"])
    lffc, hkfc, lfbc, hkbc, hvc = _proj(
        xm_c, wh_c, bh_c, lb, [(HW, "hgrn", 0), (HW, "hgrn", 1), (HW, "bf16", 0)], "proj_hgrn_ctx")

    r3 = lambda a, t: a.reshape(B, t, a.shape[-1])
    hm = _mlstm(r3(mq, T), r3(mk, T), r3(mv, T), r3(mg, T), r3(mso, T),
                r3(mkc, Tc), r3(mvc, Tc), r3(mgc, Tc), row(mlstm_norm[0]))
    hh_cm = _hgrn(r3(hq, T), r3(hkf, T), r3(hkb, T), r3(hv, T), r3(lff, T), r3(lfb, T), r3(hsg, T),
                  r3(hkfc, Tc), r3(hkbc, Tc), r3(hvc, Tc), r3(lffc, Tc), r3(lfbc, Tc),
                  row(hgrn_norm[0]))
    hh = _to_row_major(hh_cm)

    x2 = _merge(hm.reshape(B * T, MV), hh.reshape(B * T, HW), sgm, sgh, x1, ml, tpb,
                proj_m[0].astype(BF16), proj_h[0].astype(BF16), mix_w_out[0].astype(BF16))
    (out,) = _ffn(x2, ml, tpb, row(ffn2_norm[0]), w2_in, w2_out, row(final_norm),
                  rows=(6, 7, 8), rows2=(3, 4), out_x=False, aux="final")
    return out.reshape(B, T, D)
```

```python
import functools

import jax
import jax.numpy as jnp
from jax import lax
from jax.experimental import pallas as pl
from jax.experimental.pallas import tpu as pltpu

F32 = jnp.float32
BF16 = jnp.bfloat16
EPS = 1e-6

GRID_W = 64
M_HEADS = 8
M_QK_DIM = 64
M_V_DIM = 128
H_HEADS = 8
H_DIM = 128
N_MOD = 9

LANES = 128
TOKEN_TILE = 512
FFN_CHUNK = 256
PROJ_CHUNK = 512
M_CHUNK = 128
H_CHUNK = 64
HEADS_PER_STEP = 4
VMEM_LIMIT = 56 * 1024 * 1024

_NT = (((1,), (1,)), ((), ()))


def _params(sem, vmem=VMEM_LIMIT):
    return pltpu.CompilerParams(dimension_semantics=sem, vmem_limit_bytes=vmem)


def _resident(shape):
    nd = len(shape)
    return pl.BlockSpec(shape, lambda *_: (0,) * nd, pipeline_mode=pl.Buffered(1))


def _dot(a, b):
    return jnp.dot(a, b, preferred_element_type=F32)


def _rms(x, g):
    ms = jnp.mean(x * x, axis=-1, keepdims=True)
    return x * lax.rsqrt(ms + EPS) * g


def _split3(x):
    hi = x.astype(BF16)
    r = x - hi.astype(F32)
    mid = r.astype(BF16)
    lo = (r - mid.astype(F32)).astype(BF16)
    return hi, mid, lo


def _tri_cumsum(tri, x):
    hi, mid, lo = _split3(x)
    return _dot(tri, hi) + _dot(tri, mid) + _dot(tri, lo)


def _mod_kernel(c_ref, w_ref, b_ref, o_ref):
    c = c_ref[...]
    a = c * jax.nn.sigmoid(c)
    a_hi = a.astype(BF16)
    a_lo = (a - a_hi.astype(F32)).astype(BF16)
    w = w_ref[...]
    w_hi = w.astype(BF16)
    w_lo = (w - w_hi.astype(F32)).astype(BF16)
    o_ref[...] = _dot(a_hi, w_hi) + _dot(a_hi, w_lo) + _dot(a_lo, w_hi) + b_ref[...]


def _modulation(cc, w, b):
    R, D = cc.shape
    N = w.shape[1]
    tn = 1024
    return pl.pallas_call(
        _mod_kernel,
        out_shape=jax.ShapeDtypeStruct((R, N), F32),
        grid=(N // tn,),
        in_specs=[pl.BlockSpec((R, D), lambda j: (0, 0)),
                  pl.BlockSpec((D, tn), lambda j: (0, j)),
                  pl.BlockSpec((1, tn), lambda j: (0, j))],
        out_specs=pl.BlockSpec((R, tn), lambda j: (0, j)),
        compiler_params=_params(("arbitrary",)),
        name="modulation",
    )(cc, w, b)


def _ffn_kernel(x_ref, mod_ref, g_ref, win_ref, wout_ref, g2_ref, *outs,
                rows, rows2, out_x, aux, n_chunks):
    fc = FFN_CHUNK
    x = x_ref[...]
    mod = mod_ref[0]
    shift = mod[rows[0]:rows[0] + 1]
    scale = mod[rows[1]:rows[1] + 1]
    gate = mod[rows[2]:rows[2] + 1]
    xm = (_rms(x, g_ref[...]) * (1.0 + scale) + shift).astype(BF16)
    acc = jnp.zeros(x.shape, F32)
    for j in range(n_chunks):
        au = _dot(xm, win_ref[:, j * 2 * fc:(j + 1) * 2 * fc])
        a = au[:, :fc]
        u = au[:, fc:]
        h = (a * jax.nn.sigmoid(a) * u).astype(BF16)
        acc = acc + _dot(h, wout_ref[j * fc:(j + 1) * fc, :])
    xn = x + 0.5 * gate * acc
    k = 0
    if out_x:
        outs[k][...] = xn
        k += 1
    if aux == "mix":
        shift2 = mod[rows2[0]:rows2[0] + 1]
        scale2 = mod[rows2[1]:rows2[1] + 1]
        outs[k][...] = (_rms(xn, g2_ref[...]) * (1.0 + scale2) + shift2).astype(BF16)
    elif aux == "final":
        outs[k][...] = _rms(xn, g2_ref[...])


def _ffn(x2d, mod, tiles_per_mod, g, win, wout, g2, *, rows, rows2, out_x, aux):
    N, D = x2d.shape
    F = wout.shape[0]
    tm = TOKEN_TILE
    out_shape, out_specs = [], []
    if out_x:
        out_shape.append(jax.ShapeDtypeStruct((N, D), F32))
        out_specs.append(pl.BlockSpec((tm, D), lambda i: (i, 0)))
    if aux == "mix":
        out_shape.append(jax.ShapeDtypeStruct((N, D), BF16))
        out_specs.append(pl.BlockSpec((tm, D), lambda i: (i, 0)))
    elif aux == "final":
        out_shape.append(jax.ShapeDtypeStruct((N, D), F32))
        out_specs.append(pl.BlockSpec((tm, D), lambda i: (i, 0)))
    if tiles_per_mod is None:
        mod_map = lambda i: (0, 0, 0)
    else:
        mod_map = lambda i: (i // tiles_per_mod, 0, 0)
    kern = functools.partial(_ffn_kernel, rows=rows, rows2=rows2, out_x=out_x, aux=aux,
                             n_chunks=F // FFN_CHUNK)
    return pl.pallas_call(
        kern,
        out_shape=out_shape,
        grid=(N // tm,),
        in_specs=[pl.BlockSpec((tm, D), lambda i: (i, 0)),
                  pl.BlockSpec((1, N_MOD, D), mod_map),
                  _resident((1, D)),
                  _resident(win.shape),
                  _resident(wout.shape),
                  _resident((1, D))],
        out_specs=out_specs,
        compiler_params=_params(("parallel",)),
        name="ffn_" + aux,
    )(x2d, mod, g, win, wout, g2)


def _proj_kernel(x_ref, w_ref, b_ref, lb_ref, *outs, segs):
    x = x_ref[...]
    c0 = 0
    oi = 0
    for wd, kind, arg in segs:
        for s in range(0, wd, PROJ_CHUNK):
            w = min(PROJ_CHUNK, wd - s)
            y = _dot(x, w_ref[:, c0 + s:c0 + s + w]) + b_ref[:, c0 + s:c0 + s + w]
            if kind == "bf16":
                outs[oi][:, s:s + w] = y.astype(BF16)
            elif kind == "f32":
                outs[oi][:, s:s + w] = y
            elif kind == "sigmoid":
                outs[oi][:, s:s + w] = jax.nn.sigmoid(y).astype(BF16)
            elif kind == "silu":
                outs[oi][:, s:s + w] = (y * jax.nn.sigmoid(y)).astype(BF16)
            elif kind == "hgrn":
                lb = lb_ref[arg:arg + 1, s:s + w]
                f = lb + (1.0 - lb) * jax.nn.sigmoid(y)
                outs[oi][:, s:s + w] = jnp.log(f)
                outs[oi + 1][:, s:s + w] = ((1.0 - lb) * jax.nn.sigmoid(-y)).astype(BF16)
        c0 += wd
        oi += 2 if kind == "hgrn" else 1


def _proj(x2d, w, b, lb, segs, name):
    N, D = x2d.shape
    tm = TOKEN_TILE
    out_shape, out_specs = [], []
    for wd, kind, _ in segs:
        dts = {"bf16": [BF16], "f32": [F32], "sigmoid": [BF16], "silu": [BF16],
               "hgrn": [F32, BF16]}[kind]
        for dt in dts:
            out_shape.append(jax.ShapeDtypeStruct((N, wd), dt))
            out_specs.append(pl.BlockSpec((tm, wd), lambda i: (i, 0)))
    return pl.pallas_call(
        functools.partial(_proj_kernel, segs=tuple(segs)),
        out_shape=out_shape,
        grid=(N // tm,),
        in_specs=[pl.BlockSpec((tm, D), lambda i: (i, 0)),
                  _resident(w.shape),
                  _resident(b.shape),
                  _resident(lb.shape)],
        out_specs=out_specs,
        compiler_params=_params(("parallel",)),
        name=name,
    )(x2d, w, b, lb)


def _mlstm_kernel(q_ref, k_ref, v_ref, g_ref, so_ref, kc_ref, vc_ref, gc_ref, gain_ref,
                  out_ref, hacc_ref, c_ref, m_ref):
    L = M_CHUNK
    hps = HEADS_PER_STEP
    T = q_ref.shape[0]
    Tc = kc_ref.shape[0]
    nt, nct = T // L, Tc // L
    row = lax.broadcasted_iota(jnp.int32, (L, L), 0)
    col = lax.broadcasted_iota(jnp.int32, (L, L), 1)
    lane = lax.broadcasted_iota(jnp.int32, (L, LANES), 1)
    krow = lax.broadcasted_iota(jnp.int32, (LANES, 1), 0)
    ones_ext = jnp.ones((L, LANES), BF16)

    for d in (0, 1):
        tri = (col <= row) if d == 0 else (col >= row)
        trif_t = ((col >= row) if d == 0 else (col <= row)).astype(F32)
        c_ref[...] = jnp.zeros(c_ref.shape, F32)
        m_ref[...] = jnp.zeros(m_ref.shape, F32)

        def chunk(r0, gref, kref, vref, want_out, d=d, tri=tri, trif_t=trif_t):
            GT = gref[pl.ds(r0, L), :].T[:4 * hps, :]
            lf = jax.nn.log_sigmoid(GT[2 * hps:, :])
            hi, mid, lo = _split3(lf)
            Bc = sum(_dot(part.astype(F32), trif_t) for part in (hi, mid, lo))
            for p in range(hps // 2):
                k2 = kref[pl.ds(r0, L), p * LANES:(p + 1) * LANES]
                kT = k2.astype(F32).T
                C2 = c_ref[p]
                if want_out:
                    q2 = q_ref[pl.ds(r0, L), p * LANES:(p + 1) * LANES]
                    C2b = C2.astype(BF16)
                upd = None
                decay_rows = None
                for a in range(2):
                    i = 2 * p + a
                    g = hps * d + i
                    cs = slice(i * LANES, (i + 1) * LANES)
                    vext = jnp.concatenate([vref[pl.ds(r0, L), cs], ones_ext], axis=1)
                    lf_row = lf[g:g + 1, :]
                    b_row = Bc[g:g + 1, :]
                    r_row = GT[g:g + 1, :] - b_row
                    b_last = b_row[:, L - 1:L] if d == 0 else b_row[:, 0:1]
                    m_prev = m_ref[i:i + 1, 0:1]
                    if want_out:
                        headmask = (lane < M_QK_DIM) if a == 0 else (lane >= M_QK_DIM)
                        qa = jnp.where(headmask, q2, jnp.zeros_like(q2))
                        s_raw = lax.dot_general(qa, k2, _NT, preferred_element_type=F32)
                        rm = jnp.where(tri, r_row, -jnp.inf)
                        mu = jnp.maximum(m_prev, jnp.max(rm, axis=1, keepdims=True))
                        bcol = jnp.sum(jnp.where(tri, lf_row, 0.0), axis=1, keepdims=True)
                        mu_b = jnp.broadcast_to(mu, (L, L))
                        floor_b = jnp.exp(-jnp.broadcast_to(bcol + mu, (L, LANES)))
                        s = s_raw * jnp.exp(rm - mu_b)
                        nd = (jnp.concatenate([jnp.exp(m_prev - mu_b)] * 2, axis=1) * _dot(qa, C2b)
                              + _dot(s.astype(BF16), vext))
                        h = nd[:, :LANES] / jnp.maximum(jnp.abs(nd[:, LANES:]), floor_b)
                        if d == 0:
                            hacc_ref[pl.ds(r0, L), cs] = h
                        else:
                            hs = hacc_ref[pl.ds(r0, L), cs] + h
                            y = (_rms(hs, gain_ref[:, cs])
                                 * so_ref[pl.ds(r0, L), cs].astype(F32))
                            out_ref[pl.ds(r0, L), cs] = y.astype(BF16)
                    wlog = b_last + r_row
                    m_new = jnp.maximum(b_last + m_prev, jnp.max(wlog, axis=1, keepdims=True))
                    decay = jnp.exp(b_last + m_prev - m_new)
                    rowmask = (krow < M_QK_DIM) if a == 0 else (krow >= M_QK_DIM)
                    kw = jnp.where(rowmask, kT * jnp.exp(wlog - m_new), 0.0)
                    u = _dot(kw.astype(BF16).astype(F32), vext.astype(F32))
                    upd = u if upd is None else upd + u
                    dr = jnp.where(rowmask, decay, 0.0)
                    decay_rows = dr if decay_rows is None else decay_rows + dr
                    m_ref[i:i + 1, :] = jnp.broadcast_to(m_new, (1, LANES))
                c_ref[p] = decay_rows * C2 + upd

        def ctx_body(n, carry, d=d, chunk=chunk):
            c = n if d == 0 else nct - 1 - n
            chunk(pl.multiple_of(c * L, L), gc_ref, kc_ref, vc_ref, False)
            return carry

        def lat_body(n, carry, d=d, chunk=chunk):
            c = n if d == 0 else nt - 1 - n
            chunk(pl.multiple_of(c * L, L), g_ref, k_ref, v_ref, True)
            return carry

        lax.fori_loop(0, nct, ctx_body, 0)
        lax.fori_loop(0, nt, lat_body, 0, unroll=2)


def _mlstm(q, k, v, g, so, kc, vc, gc, gain):
    B, T, _ = q.shape
    Tc = kc.shape[1]
    hps = HEADS_PER_STEP
    G = M_HEADS // hps
    wq, wv = hps * M_QK_DIM, hps * M_V_DIM
    lat = lambda w: pl.BlockSpec((None, T, w), lambda b, g: (b, 0, g))
    cx = lambda w: pl.BlockSpec((None, Tc, w), lambda b, g: (b, 0, g))
    return pl.pallas_call(
        _mlstm_kernel,
        out_shape=jax.ShapeDtypeStruct((B, T, M_HEADS * M_V_DIM), BF16),
        grid=(B, G),
        in_specs=[lat(wq), lat(wq), lat(wv), lat(LANES), lat(wv),
                  cx(wq), cx(wv), cx(LANES),
                  pl.BlockSpec((1, wv), lambda b, g: (0, g))],
        out_specs=lat(wv),
        scratch_shapes=[pltpu.VMEM((T, wv), F32),
                        pltpu.VMEM((hps // 2, 2 * M_QK_DIM, 2 * LANES), F32),
                        pltpu.VMEM((8, LANES), F32)],
        compiler_params=_params(("parallel", "parallel")),
        name="mlstm_scan",
    )(q, k, v, g, so, kc, vc, gc, gain)


def _pivot_rows(b, h, off):
    L, N = b.shape
    bc = lambda r, n: jnp.broadcast_to(b[r:r + 1, :], (n, N))
    if 2 * h >= 8:
        return jnp.concatenate([bc(x + off, 2 * h) for x in range(0, L, 2 * h)], axis=0)
    sub = lax.broadcasted_iota(jnp.int32, (8, N), 0)
    tiles = []
    for x in range(0, L, 8):
        t = bc(x + 8 - 2 * h + off, 8)
        for y in range(8 - 4 * h, -1, -2 * h):
            t = jnp.where(sub < y + 2 * h, bc(x + y + off, 8), t)
        tiles.append(t)
    return jnp.concatenate(tiles, axis=0)


def _hgrn_kernel(q_ref, kf_ref, kb_ref, v_ref, lff_ref, lfb_ref, sg_ref,
                 kfc_ref, kbc_ref, vc_ref, lffc_ref, lfbc_ref, gain_ref,
                 out_ref, hacc_ref, st_ref):
    L = H_CHUNK
    hps = HEADS_PER_STEP
    T = q_ref.shape[0]
    Tc = vc_ref.shape[0]
    nt, nct = T // L, Tc // L
    row = lax.broadcasted_iota(jnp.int32, (L, L), 0)
    col = lax.broadcasted_iota(jnp.int32, (L, L), 1)
    tok = lax.broadcasted_iota(jnp.int32, (L, H_DIM), 0)
    halves = [L >> (n + 1) for n in range(L.bit_length() - 1)]
    zpad_f = jnp.zeros((H_DIM - L, H_DIM), F32)
    zpad_b = jnp.zeros((H_DIM - L, H_DIM), BF16)

    for d in (0, 1):
        tri = (col <= row) if d == 0 else (col >= row)
        trib = tri.astype(BF16)
        st_ref[...] = jnp.zeros(st_ref.shape, F32)
        k_lat, lf_lat = (kf_ref, lff_ref) if d == 0 else (kb_ref, lfb_ref)
        k_ctx, lf_ctx = (kfc_ref, lffc_ref) if d == 0 else (kbc_ref, lfbc_ref)
        q_side, pair = [], []
        for h in halves:
            late = (tok & (2 * h - 1)) >= h
            q_side.append(late if d == 0 else ~late)
            same = (row & -(2 * h)) == (col & -(2 * h))
            r_late = (row & (2 * h - 1)) >= h
            c_late = (col & (2 * h - 1)) >= h
            pair.append(same & (r_late & ~c_late if d == 0 else ~r_late & c_late))

        def chunk(r0, kref, lfref, vref, want_out, d=d, trib=trib, q_side=q_side, pair=pair):
            rows = pl.ds(r0, L)
            lf = lfref[rows, :]
            b = _tri_cumsum(trib, lf)
            blast = b[L - 1:L, :] if d == 0 else b[0:1, :]
            for i in range(hps):
                cs = slice(i * H_DIM, (i + 1) * H_DIM)
                b_h = b[:, cs]
                lf_h = lf[:, cs]
                bl = blast[:, cs]
                k_h = kref[rows, cs].astype(F32)
                v_h = vref[rows, cs]
                ST = st_ref[i]
                if want_out:
                    q_h = q_ref[rows, cs].astype(F32)
                    attn = jnp.zeros((L, L), F32)
                    for n, h in enumerate(halves):
                        if h > 1:
                            z = jnp.exp(-jnp.abs(b_h - _pivot_rows(b_h, h, h - 1 + d)))
                            x = (jnp.where(q_side[n], q_h, k_h) * z).astype(BF16)
                        else:
                            x = jnp.where(q_side[n], q_h * jnp.exp(lf_h), k_h).astype(BF16)
                        qk = lax.dot_general(x, x, _NT, preferred_element_type=F32)
                        attn = attn + jnp.where(pair[n], qk, 0.0)
                    qc = (q_h * jnp.exp(b_h)).astype(BF16)
                    o = (lax.dot_general(qc, ST.astype(BF16), _NT, preferred_element_type=F32)
                         + _dot(attn.astype(BF16), v_h)
                         + jnp.sum(q_h * k_h, axis=1, keepdims=True) * v_h.astype(F32))
                    if d == 0:
                        hacc_ref[rows, cs] = o
                    else:
                        hs = hacc_ref[rows, cs] + o
                        y = _rms(hs, gain_ref[:, cs]) * sg_ref[rows, cs].astype(F32)
                        out_ref[rows, cs] = y.astype(BF16)
                kd = jnp.concatenate([(k_h * jnp.exp(bl - b_h)).astype(BF16), zpad_b], axis=0)
                vT = jnp.concatenate([v_h.astype(F32), zpad_f], axis=0).T.astype(BF16)
                st_ref[i] = ST * jnp.exp(bl) + _dot(vT, kd)

        def ctx_body(n, carry, d=d, chunk=chunk, k_ctx=k_ctx, lf_ctx=lf_ctx):
            c = n if d == 0 else nct - 1 - n
            chunk(pl.multiple_of(c * L, L), k_ctx, lf_ctx, vc_ref, False)
            return carry

        def lat_body(n, carry, d=d, chunk=chunk, k_lat=k_lat, lf_lat=lf_lat):
            c = n if d == 0 else nt - 1 - n
            chunk(pl.multiple_of(c * L, L), k_lat, lf_lat, v_ref, True)
            return carry

        lax.fori_loop(0, nct, ctx_body, 0)
        lax.fori_loop(0, nt, lat_body, 0, unroll=2)


def _hgrn(q, kf, kb, v, lff, lfb, sg, kfc, kbc, vc, lffc, lfbc, gain):
    B, T, W = q.shape
    Tc = vc.shape[1]
    hps = HEADS_PER_STEP
    G = H_HEADS // hps
    w = hps * H_DIM
    lat = pl.BlockSpec((None, T, w), lambda b, g: (b, 0, g))
    cx = pl.BlockSpec((None, Tc, w), lambda b, g: (b, 0, g))
    return pl.pallas_call(
        _hgrn_kernel,
        out_shape=jax.ShapeDtypeStruct((B, T, W), BF16),
        grid=(B, G),
        in_specs=[lat] * 7 + [cx] * 5 + [pl.BlockSpec((1, w), lambda b, g: (0, g))],
        out_specs=lat,
        scratch_shapes=[pltpu.VMEM((T, w), F32),
                        pltpu.VMEM((hps, H_DIM, H_DIM), F32)],
        compiler_params=_params(("parallel", "parallel")),
        name="hgrn_scan",
    )(q, kf, kb, v, lff, lfb, sg, kfc, kbc, vc, lffc, lfbc, gain)


def _merge_kernel(hm_ref, hh_ref, gm_ref, gh_ref, x_ref, mod_ref, pm_ref, ph_ref, wo_ref,
                  out_ref, *, gate_row):
    a = _dot(hm_ref[...], pm_ref[...])
    b = _dot(hh_ref[...], ph_ref[...])
    z = (gm_ref[...].astype(F32) * a + gh_ref[...].astype(F32) * b).astype(BF16)
    y = _dot(z, wo_ref[...])
    gate = mod_ref[0][gate_row:gate_row + 1]
    out_ref[...] = x_ref[...] + gate * y


def _merge(hm, hh, gm, gh, x2d, mod, tiles_per_mod, pm, ph, wo):
    N, D = x2d.shape
    tm = TOKEN_TILE
    tile = pl.BlockSpec((tm, D), lambda i: (i, 0))
    return pl.pallas_call(
        functools.partial(_merge_kernel, gate_row=5),
        out_shape=jax.ShapeDtypeStruct((N, D), F32),
        grid=(N // tm,),
        in_specs=[tile, tile, tile, tile, tile,
                  pl.BlockSpec((1, N_MOD, D), lambda i: (i // tiles_per_mod, 0, 0)),
                  _resident(pm.shape), _resident(ph.shape), _resident(wo.shape)],
        out_specs=tile,
        compiler_params=_params(("parallel",)),
        name="merge",
    )(hm, hh, gm, gh, x2d, mod, pm, ph, wo)


def _interleave_ffn_in(w):
    D, F2 = w.shape
    F = F2 // 2
    a = w[:, :F].reshape(D, F // FFN_CHUNK, 1, FFN_CHUNK)
    u = w[:, F:].reshape(D, F // FFN_CHUNK, 1, FFN_CHUNK)
    return jnp.concatenate([a, u], axis=2).reshape(D, F2)


def _to_col_major(a):
    B, T, C = a.shape
    return a.reshape(B, T // GRID_W, GRID_W, C).transpose(0, 2, 1, 3).reshape(B, T, C)


def _to_row_major(a):
    B, T, C = a.shape
    return a.reshape(B, GRID_W, T // GRID_W, C).transpose(0, 2, 1, 3).reshape(B, T, C)


def kernel(x, c, ctx, c_ctx, ada_w, ada_b, ffn1_norm, ffn1_w_in, ffn1_w_out, mix_norm,
           mix_w_in, mix_b_in, mlstm_norm, hgrn_lb_logits, hgrn_norm, proj_m, proj_h,
           mix_w_out, ffn2_norm, ffn2_w_in, ffn2_w_out, final_norm):
    B, T, D = x.shape
    Tc = ctx.shape[1]
    hps = HEADS_PER_STEP
    MQ, MV, HW = M_HEADS * M_QK_DIM, M_HEADS * M_V_DIM, H_HEADS * H_DIM

    pad = (-(B + 1)) % 8
    cc = jnp.concatenate([c, c_ctx[None, :], jnp.zeros((pad, D), F32)], axis=0)
    mod = _modulation(cc, ada_w[0], ada_b[0][None, :])
    ml = mod[:B].reshape(B, N_MOD, D)
    mc = mod[B:B + 1].reshape(1, N_MOD, D)

    row = lambda v: v.reshape(1, -1).astype(F32)
    w1_in = _interleave_ffn_in(ffn1_w_in[0]).astype(BF16)
    w1_out = ffn1_w_out[0].astype(BF16)
    w2_in = _interleave_ffn_in(ffn2_w_in[0]).astype(BF16)
    w2_out = ffn2_w_out[0].astype(BF16)

    W, bias = mix_w_in[0], mix_b_in[0]
    o = 0
    seg = {}
    for name, wd in (("mq", MQ), ("mk", MQ), ("mv", MV), ("mo", MV), ("ig", 2 * M_HEADS),
                     ("fg", 2 * M_HEADS), ("hq", HW), ("hff", HW), ("hfb", HW), ("hi", HW),
                     ("hg", HW), ("gm", D), ("gh", D)):
        seg[name] = (o, o + wd)
        o += wd
    cols = lambda name: jnp.arange(*seg[name])
    gate_cols = []
    for g in range(M_HEADS // hps):
        heads = jnp.arange(g * hps, (g + 1) * hps)
        gate_cols.append(jnp.concatenate([
            seg["ig"][0] + heads, seg["ig"][0] + M_HEADS + heads,
            seg["fg"][0] + heads, seg["fg"][0] + M_HEADS + heads]))

    def gather(parts, scale_q=False):
        ws, bs = [], []
        for p in parts:
            if p == "gates":
                for gc in gate_cols:
                    ws += [W[:, gc], jnp.zeros((D, LANES - 4 * hps), F32)]
                    bs += [bias[gc], jnp.zeros((LANES - 4 * hps,), F32)]
            else:
                s = M_QK_DIM ** -0.5 if p == "mq" else 1.0
                ws.append(W[:, seg[p][0]:seg[p][1]] * s)
                bs.append(bias[seg[p][0]:seg[p][1]] * s)
        return jnp.concatenate(ws, axis=1).astype(BF16), jnp.concatenate(bs)[None, :]

    n_gate = LANES * (M_HEADS // hps)
    lb = jnp.cumsum(jax.nn.softmax(hgrn_lb_logits.astype(F32), axis=1), axis=1)[:, 0]

    x2d = x.reshape(B * T, D)
    c2d = ctx.reshape(B * Tc, D)
    tpb = T // TOKEN_TILE
    (xm_c,) = _ffn(c2d, mc, None, row(ffn1_norm[0]), w1_in, w1_out, row(mix_norm[0]),
                   rows=(0, 1, 2), rows2=(3, 4), out_x=False, aux="mix")
    x1, xm_l = _ffn(x2d, ml, tpb, row(ffn1_norm[0]), w1_in, w1_out, row(mix_norm[0]),
                    rows=(0, 1, 2), rows2=(3, 4), out_x=True, aux="mix")

    wm_l, bm_l = gather(["mq", "mk", "mv", "mo", "gates", "gm", "gh"])
    mq, mk, mv, mso, mg, sgm, sgh = _proj(
        xm_l, wm_l, bm_l, lb,
        [(MQ, "bf16", 0), (MQ, "bf16", 0), (MV, "bf16", 0), (MV, "sigmoid", 0),
         (n_gate, "f32", 0), (D, "sigmoid", 0), (D, "sigmoid", 0)], "proj_mlstm")
    wm_c, bm_c = gather(["mk", "mv", "gates"])
    mkc, mvc, mgc = _proj(xm_c, wm_c, bm_c, lb,
                          [(MQ, "bf16", 0), (MV, "bf16", 0), (n_gate, "f32", 0)], "proj_mlstm_ctx")

    xm_cm = _to_col_major(xm_l.reshape(B, T, D)).reshape(B * T, D)
    wh_l, bh_l = gather(["hq", "hff", "hfb", "hi", "hg"])
    hq, lff, hkf, lfb, hkb, hv, hsg = _proj(
        xm_cm, wh_l, bh_l, lb,
        [(HW, "silu", 0), (HW, "hgrn", 0), (HW, "hgrn", 1), (HW, "bf16", 0), (HW, "silu", 0)],
        "proj_hgrn")
    wh_c, bh_c = gather(["hff", "hfb", "hi"])
    lffc, hkfc, lfbc, hkbc, hvc = _proj(
        xm_c, wh_c, bh_c, lb, [(HW, "hgrn", 0), (HW, "hgrn", 1), (HW, "bf16", 0)], "proj_hgrn_ctx")

    r3 = lambda a, t: a.reshape(B, t, a.shape[-1])
    hm = _mlstm(r3(mq, T), r3(mk, T), r3(mv, T), r3(mg, T), r3(mso, T),
                r3(mkc, Tc), r3(mvc, Tc), r3(mgc, Tc), row(mlstm_norm[0]))
    hh_cm = _hgrn(r3(hq, T), r3(hkf, T), r3(hkb, T), r3(hv, T), r3(lff, T), r3(lfb, T), r3(hsg, T),
                  r3(hkfc, Tc), r3(hkbc, Tc), r3(hvc, Tc), r3(lffc, Tc), r3(lfbc, Tc),
                  row(hgrn_norm[0]))
    hh = _to_row_major(hh_cm)

    x2 = _merge(hm.reshape(B * T, MV), hh.reshape(B * T, HW), sgm, sgh, x1, ml, tpb,
                proj_m[0].astype(BF16), proj_h[0].astype(BF16), mix_w_out[0].astype(BF16))
    (out,) = _ffn(x2, ml, tpb, row(ffn2_norm[0]), w2_in, w2_out, row(final_norm),
                  rows=(6, 7, 8), rows2=(3, 4), out_x=False, aux="final")
    return out.reshape(B, T, D)
```

```python
import functools

import jax
import jax.numpy as jnp
from jax import lax
from jax.experimental import pallas as pl
from jax.experimental.pallas import tpu as pltpu

F32 = jnp.float32
BF16 = jnp.bfloat16
EPS = 1e-6

GRID_W = 64
M_HEADS = 8
M_QK_DIM = 64
M_V_DIM = 128
H_HEADS = 8
H_DIM = 128
N_MOD = 9

LANES = 128
TOKEN_TILE = 512
FFN_CHUNK = 256
PROJ_CHUNK = 512
M_CHUNK = 128
H_CHUNK = 64
H_BLOCK = 16
H_BLOCK_MAX_EXPONENT = 64.0
HEADS_PER_STEP = 4
VMEM_LIMIT = 56 * 1024 * 1024

_NT = (((1,), (1,)), ((), ()))


def _params(sem, vmem=VMEM_LIMIT):
    return pltpu.CompilerParams(dimension_semantics=sem, vmem_limit_bytes=vmem)


def _resident(shape):
    nd = len(shape)
    return pl.BlockSpec(shape, lambda *_: (0,) * nd, pipeline_mode=pl.Buffered(1))


def _dot(a, b):
    return jnp.dot(a, b, preferred_element_type=F32)


def _rms(x, g):
    ms = jnp.mean(x * x, axis=-1, keepdims=True)
    return x * lax.rsqrt(ms + EPS) * g


def _split3(x):
    hi = x.astype(BF16)
    r = x - hi.astype(F32)
    mid = r.astype(BF16)
    lo = (r - mid.astype(F32)).astype(BF16)
    return hi, mid, lo


def _tri_cumsum(tri, x):
    hi, mid, lo = _split3(x)
    return _dot(tri, hi) + _dot(tri, mid) + _dot(tri, lo)


def _mod_kernel(c_ref, w_ref, b_ref, o_ref):
    c = c_ref[...]
    a = c * jax.nn.sigmoid(c)
    a_hi = a.astype(BF16)
    a_lo = (a - a_hi.astype(F32)).astype(BF16)
    w = w_ref[...]
    w_hi = w.astype(BF16)
    w_lo = (w - w_hi.astype(F32)).astype(BF16)
    o_ref[...] = _dot(a_hi, w_hi) + _dot(a_hi, w_lo) + _dot(a_lo, w_hi) + b_ref[...]


def _modulation(cc, w, b):
    R, D = cc.shape
    N = w.shape[1]
    tn = 1024
    return pl.pallas_call(
        _mod_kernel,
        out_shape=jax.ShapeDtypeStruct((R, N), F32),
        grid=(N // tn,),
        in_specs=[pl.BlockSpec((R, D), lambda j: (0, 0)),
                  pl.BlockSpec((D, tn), lambda j: (0, j)),
                  pl.BlockSpec((1, tn), lambda j: (0, j))],
        out_specs=pl.BlockSpec((R, tn), lambda j: (0, j)),
        compiler_params=_params(("arbitrary",)),
        name="modulation",
    )(cc, w, b)


def _ffn_kernel(x_ref, mod_ref, g_ref, win_ref, wout_ref, g2_ref, *outs,
                rows, rows2, out_x, aux, n_chunks):
    fc = FFN_CHUNK
    x = x_ref[...]
    mod = mod_ref[0]
    shift = mod[rows[0]:rows[0] + 1]
    scale = mod[rows[1]:rows[1] + 1]
    gate = mod[rows[2]:rows[2] + 1]
    xm = (_rms(x, g_ref[...]) * (1.0 + scale) + shift).astype(BF16)
    acc = jnp.zeros(x.shape, F32)
    for j in range(n_chunks):
        au = _dot(xm, win_ref[:, j * 2 * fc:(j + 1) * 2 * fc])
        a = au[:, :fc]
        u = au[:, fc:]
        h = (a * jax.nn.sigmoid(a) * u).astype(BF16)
        acc = acc + _dot(h, wout_ref[j * fc:(j + 1) * fc, :])
    xn = x + 0.5 * gate * acc
    k = 0
    if out_x:
        outs[k][...] = xn
        k += 1
    if aux == "mix":
        shift2 = mod[rows2[0]:rows2[0] + 1]
        scale2 = mod[rows2[1]:rows2[1] + 1]
        outs[k][...] = (_rms(xn, g2_ref[...]) * (1.0 + scale2) + shift2).astype(BF16)
    elif aux == "final":
        outs[k][...] = _rms(xn, g2_ref[...])


def _ffn(x2d, mod, tiles_per_mod, g, win, wout, g2, *, rows, rows2, out_x, aux):
    N, D = x2d.shape
    F = wout.shape[0]
    tm = TOKEN_TILE
    out_shape, out_specs = [], []
    if out_x:
        out_shape.append(jax.ShapeDtypeStruct((N, D), F32))
        out_specs.append(pl.BlockSpec((tm, D), lambda i: (i, 0)))
    if aux == "mix":
        out_shape.append(jax.ShapeDtypeStruct((N, D), BF16))
        out_specs.append(pl.BlockSpec((tm, D), lambda i: (i, 0)))
    elif aux == "final":
        out_shape.append(jax.ShapeDtypeStruct((N, D), F32))
        out_specs.append(pl.BlockSpec((tm, D), lambda i: (i, 0)))
    if tiles_per_mod is None:
        mod_map = lambda i: (0, 0, 0)
    else:
        mod_map = lambda i: (i // tiles_per_mod, 0, 0)
    kern = functools.partial(_ffn_kernel, rows=rows, rows2=rows2, out_x=out_x, aux=aux,
                             n_chunks=F // FFN_CHUNK)
    return pl.pallas_call(
        kern,
        out_shape=out_shape,
        grid=(N // tm,),
        in_specs=[pl.BlockSpec((tm, D), lambda i: (i, 0)),
                  pl.BlockSpec((1, N_MOD, D), mod_map),
                  _resident((1, D)),
                  _resident(win.shape),
                  _resident(wout.shape),
                  _resident((1, D))],
        out_specs=out_specs,
        compiler_params=_params(("parallel",)),
        name="ffn_" + aux,
    )(x2d, mod, g, win, wout, g2)


def _proj_kernel(x_ref, w_ref, b_ref, lb_ref, *outs, segs):
    x = x_ref[...]
    c0 = 0
    oi = 0
    for wd, kind, arg in segs:
        for s in range(0, wd, PROJ_CHUNK):
            w = min(PROJ_CHUNK, wd - s)
            y = _dot(x, w_ref[:, c0 + s:c0 + s + w]) + b_ref[:, c0 + s:c0 + s + w]
            if kind == "bf16":
                outs[oi][:, s:s + w] = y.astype(BF16)
            elif kind == "f32":
                outs[oi][:, s:s + w] = y
            elif kind == "sigmoid":
                outs[oi][:, s:s + w] = jax.nn.sigmoid(y).astype(BF16)
            elif kind == "silu":
                outs[oi][:, s:s + w] = (y * jax.nn.sigmoid(y)).astype(BF16)
            elif kind == "hgrn":
                lb = lb_ref[arg:arg + 1, s:s + w]
                f = lb + (1.0 - lb) * jax.nn.sigmoid(y)
                outs[oi][:, s:s + w] = jnp.log(f)
                outs[oi + 1][:, s:s + w] = ((1.0 - lb) * jax.nn.sigmoid(-y)).astype(BF16)
        c0 += wd
        oi += 2 if kind == "hgrn" else 1


def _proj(x2d, w, b, lb, segs, name):
    N, D = x2d.shape
    tm = TOKEN_TILE
    out_shape, out_specs = [], []
    for wd, kind, _ in segs:
        dts = {"bf16": [BF16], "f32": [F32], "sigmoid": [BF16], "silu": [BF16],
               "hgrn": [F32, BF16]}[kind]
        for dt in dts:
            out_shape.append(jax.ShapeDtypeStruct((N, wd), dt))
            out_specs.append(pl.BlockSpec((tm, wd), lambda i: (i, 0)))
    return pl.pallas_call(
        functools.partial(_proj_kernel, segs=tuple(segs)),
        out_shape=out_shape,
        grid=(N // tm,),
        in_specs=[pl.BlockSpec((tm, D), lambda i: (i, 0)),
                  _resident(w.shape),
                  _resident(b.shape),
                  _resident(lb.shape)],
        out_specs=out_specs,
        compiler_params=_params(("parallel",)),
        name=name,
    )(x2d, w, b, lb)


def _mlstm_kernel(q_ref, k_ref, v_ref, g_ref, so_ref, kc_ref, vc_ref, gc_ref, gain_ref,
                  out_ref, hacc_ref, c_ref, m_ref, rows_ref,
                  s0_ref, wi0_ref, fl0_ref, s1_ref, wi1_ref, fl1_ref):
    L = M_CHUNK
    hps = HEADS_PER_STEP
    T = q_ref.shape[0]
    Tc = kc_ref.shape[0]
    nt, nct = T // L, Tc // L
    assert nt % 2 == 0
    row = lax.broadcasted_iota(jnp.int32, (L, L), 0)
    col = lax.broadcasted_iota(jnp.int32, (L, L), 1)
    lane = lax.broadcasted_iota(jnp.int32, (L, LANES), 1)
    krow = lax.broadcasted_iota(jnp.int32, (LANES, 1), 0)
    ones_ext = jnp.ones((L, LANES), BF16)
    R_R, R_LF, R_W, R_M, R_DEC = (k * hps for k in range(5))

    for d in (0, 1):
        tri = (col <= row) if d == 0 else (col >= row)
        trif_t = ((col >= row) if d == 0 else (col <= row)).astype(F32)
        c_ref[...] = jnp.zeros(c_ref.shape, F32)
        m_ref[...] = jnp.zeros(m_ref.shape, F32)

        def gate_body(n, carry, gref, slot0, d=d, trif_t=trif_t):
            GT = gref[pl.ds(pl.multiple_of(n * L, L), L), :].T[:4 * hps, :]
            lf = jax.nn.log_sigmoid(GT[(2 + d) * hps:(3 + d) * hps, :])
            b = sum(_dot(part.astype(F32), trif_t) for part in _split3(lf))
            r = GT[d * hps:(d + 1) * hps, :] - b
            b_last = b[:, L - 1:L] if d == 0 else b[:, 0:1]
            wlog = b_last + r
            wide = lambda x: jnp.broadcast_to(x, (hps, L))
            rows_ref[slot0 + n] = jnp.concatenate(
                [r, lf, wlog, wide(b_last), wide(jnp.max(wlog, axis=1, keepdims=True)),
                 jnp.zeros((rows_ref.shape[1] - 5 * hps, L), F32)], axis=0)
            return carry

        def stab_body(n, carry, nchunks, slot0, d=d):
            idx = slot0 + (n if d == 0 else nchunks - 1 - n)
            R = rows_ref[idx]
            m_prev = m_ref[0:hps, 0:1]
            bm = R[R_M:R_M + hps, 0:1] + m_prev
            m_new = jnp.maximum(bm, R[R_DEC:R_DEC + hps, 0:1])
            wide = lambda x: jnp.broadcast_to(x, (hps, L))
            rows_ref[idx, R_W:R_W + 3 * hps, :] = jnp.concatenate(
                [jnp.exp(R[R_W:R_W + hps, :] - m_new), wide(m_prev), wide(jnp.exp(bm - m_new))], axis=0)
            m_ref[0:hps, :] = jnp.broadcast_to(m_new, (hps, LANES))
            return carry

        lax.fori_loop(0, nct, functools.partial(gate_body, gref=gc_ref, slot0=0), 0, unroll=True)
        lax.fori_loop(0, nt, functools.partial(gate_body, gref=g_ref, slot0=nct), 0, unroll=4)
        lax.fori_loop(0, nct, functools.partial(stab_body, nchunks=nct, slot0=0), 0)
        lax.fori_loop(0, nt, functools.partial(stab_body, nchunks=nt, slot0=nct), 0)

        def update_state(p, R, rows, kref, vref):
            kT = kref[rows, p * LANES:(p + 1) * LANES].astype(F32).T
            upd = None
            decay_rows = None
            for a in range(2):
                i = 2 * p + a
                vext = jnp.concatenate([vref[rows, i * LANES:(i + 1) * LANES], ones_ext], axis=1)
                rowmask = (krow < M_QK_DIM) if a == 0 else (krow >= M_QK_DIM)
                kw = jnp.where(rowmask, kT * R[R_W + i:R_W + i + 1, :], 0.0)
                u = _dot(kw.astype(BF16).astype(F32), vext.astype(F32))
                upd = u if upd is None else upd + u
                dr = jnp.where(rowmask, R[R_DEC + i:R_DEC + i + 1, 0:1], 0.0)
                decay_rows = dr if decay_rows is None else decay_rows + dr
            c_ref[p] = decay_rows * c_ref[p] + upd

        def ctx_body(n, carry, d=d, update_state=update_state):
            c = n if d == 0 else nct - 1 - n
            rows = pl.ds(pl.multiple_of(c * L, L), L)
            for p in range(hps // 2):
                update_state(p, rows_ref[c], rows, kc_ref, vc_ref)
            return carry

        lax.fori_loop(0, nct, ctx_body, 0)

        def head_q(rows, p, a):
            q2 = q_ref[rows, p * LANES:(p + 1) * LANES]
            headmask = (lane < M_QK_DIM) if a == 0 else (lane >= M_QK_DIM)
            return jnp.where(headmask, q2, jnp.zeros_like(q2))

        def pair_weights(c, s_ref, wi_ref, fl_ref, tri=tri, head_q=head_q):
            rows = pl.ds(pl.multiple_of(c * L, L), L)
            R = rows_ref[nct + c]
            for i in range(hps):
                p, a = divmod(i, 2)
                s_raw = lax.dot_general(head_q(rows, p, a), k_ref[rows, p * LANES:(p + 1) * LANES],
                                        _NT, preferred_element_type=F32)
                m_prev = R[R_M + i:R_M + i + 1, 0:1]
                rm = jnp.where(tri, R[R_R + i:R_R + i + 1, :], -jnp.inf)
                mu = jnp.maximum(m_prev, jnp.max(rm, axis=1, keepdims=True))
                bcol = jnp.sum(jnp.where(tri, R[R_LF + i:R_LF + i + 1, :], 0.0), axis=1, keepdims=True)
                mu_b = jnp.broadcast_to(mu, (L, L))
                s_ref[i] = s_raw * jnp.exp(rm - mu_b)
                wi_ref[i] = jnp.exp(m_prev - mu_b)
                fl_ref[i] = jnp.exp(-jnp.broadcast_to(bcol + mu, (L, LANES)))

        def emit(c, s_ref, wi_ref, fl_ref, d=d, head_q=head_q, update_state=update_state):
            rows = pl.ds(pl.multiple_of(c * L, L), L)
            for p in range(hps // 2):
                C2b = c_ref[p].astype(BF16)
                for a in range(2):
                    i = 2 * p + a
                    cs = slice(i * LANES, (i + 1) * LANES)
                    vext = jnp.concatenate([v_ref[rows, cs], ones_ext], axis=1)
                    nd = (jnp.concatenate([wi_ref[i]] * 2, axis=1) * _dot(head_q(rows, p, a), C2b)
                          + _dot(s_ref[i].astype(BF16), vext))
                    h = nd[:, :LANES] / jnp.maximum(jnp.abs(nd[:, LANES:]), fl_ref[i])
                    if d == 0:
                        hacc_ref[rows, cs] = h
                    else:
                        hs = hacc_ref[rows, cs] + h
                        y = _rms(hs, gain_ref[:, cs]) * so_ref[rows, cs].astype(F32)
                        out_ref[rows, cs] = y.astype(BF16)
                update_state(p, rows_ref[nct + c], rows, k_ref, v_ref)

        order = (lambda n: n) if d == 0 else (lambda n: nt - 1 - n)
        slot0 = (s0_ref, wi0_ref, fl0_ref)
        slot1 = (s1_ref, wi1_ref, fl1_ref)

        def lat_body(n, carry, pair_weights=pair_weights, emit=emit, order=order):
            c0, c1 = 2 * n, 2 * n + 1
            c2 = jnp.minimum(2 * n + 2, nt - 1)
            pair_weights(order(c1), *slot1)
            emit(order(c0), *slot0)
            pair_weights(order(c2), *slot0)
            emit(order(c1), *slot1)
            return carry

        pair_weights(order(0), *slot0)
        lax.fori_loop(0, nt // 2, lat_body, 0)


def _mlstm(q, k, v, g, so, kc, vc, gc, gain):
    B, T, _ = q.shape
    Tc = kc.shape[1]
    hps = HEADS_PER_STEP
    G = M_HEADS // hps
    wq, wv = hps * M_QK_DIM, hps * M_V_DIM
    lat = lambda w: pl.BlockSpec((None, T, w), lambda b, g: (b, 0, g))
    cx = lambda w: pl.BlockSpec((None, Tc, w), lambda b, g: (b, 0, g))
    return pl.pallas_call(
        _mlstm_kernel,
        out_shape=jax.ShapeDtypeStruct((B, T, M_HEADS * M_V_DIM), BF16),
        grid=(B, G),
        in_specs=[lat(wq), lat(wq), lat(wv), lat(LANES), lat(wv),
                  cx(wq), cx(wv), cx(LANES),
                  pl.BlockSpec((1, wv), lambda b, g: (0, g))],
        out_specs=lat(wv),
        scratch_shapes=[pltpu.VMEM((T, wv), F32),
                        pltpu.VMEM((hps // 2, 2 * M_QK_DIM, 2 * LANES), F32),
                        pltpu.VMEM((8, LANES), F32),
                        pltpu.VMEM(((T + Tc) // M_CHUNK, 8 * hps, M_CHUNK), F32)]
                       + [pltpu.VMEM((hps, M_CHUNK, M_CHUNK), F32)] * 6,
        compiler_params=_params(("parallel", "parallel")),
        name="mlstm_scan",
    )(q, k, v, g, so, kc, vc, gc, gain)


def _pivot_rows(b, h, off):
    L, N = b.shape
    bc = lambda r, n: jnp.broadcast_to(b[r:r + 1, :], (n, N))
    if 2 * h >= 8:
        return jnp.concatenate([bc(x + off, 2 * h) for x in range(0, L, 2 * h)], axis=0)
    sub = lax.broadcasted_iota(jnp.int32, (8, N), 0)
    tiles = []
    for x in range(0, L, 8):
        t = bc(x + 8 - 2 * h + off, 8)
        for y in range(8 - 4 * h, -1, -2 * h):
            t = jnp.where(sub < y + 2 * h, bc(x + y + off, 8), t)
        tiles.append(t)
    return jnp.concatenate(tiles, axis=0)


def _block_entry_rows(b, beta, d):
    L, N = b.shape
    tiles = []
    for x in range(0, L, beta):
        r = x - 1 if d == 0 else x + beta
        inside = 0 <= r < L
        tiles.append(jnp.broadcast_to(b[r:r + 1, :], (beta, N)) if inside else jnp.zeros((beta, N), F32))
    return jnp.concatenate(tiles, axis=0)


def _hgrn_kernel(q_ref, kf_ref, kb_ref, v_ref, lff_ref, lfb_ref, sg_ref,
                 kfc_ref, kbc_ref, vc_ref, lffc_ref, lfbc_ref, gain_ref,
                 out_ref, hacc_ref, st_ref, b_ref, bc_ref, attn0_ref, attn1_ref, *, beta):
    L = H_CHUNK
    hps = HEADS_PER_STEP
    T = q_ref.shape[0]
    Tc = vc_ref.shape[0]
    nt, nct = T // L, Tc // L
    assert nt % 2 == 0
    row = lax.broadcasted_iota(jnp.int32, (L, L), 0)
    col = lax.broadcasted_iota(jnp.int32, (L, L), 1)
    halves = [h for h in (L >> (n + 1) for n in range(L.bit_length() - 1)) if h >= beta]
    zpad_f = jnp.zeros((H_DIM - L, H_DIM), F32)
    zpad_b = jnp.zeros((H_DIM - L, H_DIM), BF16)
    neg_log2e = -1.4426950408889634

    for d in (0, 1):
        tri = (col <= row) if d == 0 else (col >= row)
        trib = tri.astype(BF16)
        last = L - 1 if d == 0 else 0
        st_ref[...] = jnp.zeros(st_ref.shape, F32)
        k_lat, lf_lat = (kf_ref, lff_ref) if d == 0 else (kb_ref, lfb_ref)
        k_ctx, lf_ctx = (kfc_ref, lffc_ref) if d == 0 else (kbc_ref, lfbc_ref)
        local = tri & ((row & -beta) == (col & -beta))
        pair = []
        for h in halves:
            same = (row & -(2 * h)) == (col & -(2 * h))
            r_late = (row & (2 * h - 1)) >= h
            c_late = (col & (2 * h - 1)) >= h
            pair.append(same & (r_late & ~c_late if d == 0 else ~r_late & c_late))

        def cumsum_body(n, carry, lfref, bref, trib=trib):
            rows = pl.ds(pl.multiple_of(n * L, L), L)
            bref[rows, :] = _tri_cumsum(trib, lfref[rows, :])
            return carry

        lax.fori_loop(0, nct, functools.partial(cumsum_body, lfref=lf_ctx, bref=bc_ref), 0)
        lax.fori_loop(0, nt, functools.partial(cumsum_body, lfref=lf_lat, bref=b_ref), 0, unroll=2)

        def update_state(i, rows, cs, kref, vref, bref):
            b_h = bref[rows, cs]
            bl = b_h[last:last + 1, :]
            kd = (kref[rows, cs].astype(F32) * jnp.exp(bl - b_h)).astype(BF16)
            vT = jnp.concatenate([vref[rows, cs].astype(F32), zpad_f], axis=0).T.astype(BF16)
            st_ref[i] = (st_ref[i] * jnp.exp(bl)
                         + _dot(vT, jnp.concatenate([kd, zpad_b], axis=0)))

        def ctx_body(n, carry, d=d, k_ctx=k_ctx, update_state=update_state):
            c = n if d == 0 else nct - 1 - n
            rows = pl.ds(pl.multiple_of(c * L, L), L)
            for i in range(hps):
                update_state(i, rows, slice(i * H_DIM, (i + 1) * H_DIM), k_ctx, vc_ref, bc_ref)
            return carry

        lax.fori_loop(0, nct, ctx_body, 0)

        def pair_weights(c, attn_ref, d=d, k_lat=k_lat, local=local, pair=pair):
            rows = pl.ds(pl.multiple_of(c * L, L), L)
            for i in range(hps):
                cs = slice(i * H_DIM, (i + 1) * H_DIM)
                b_h = b_ref[rows, cs]
                q_h = q_ref[rows, cs].astype(F32)
                k_h = k_lat[rows, cs].astype(F32)
                if beta > 1:
                    dl = (b_h - _block_entry_rows(b_h, beta, d)) * neg_log2e
                    qs = (q_h * jnp.exp2(-dl)).astype(BF16)
                    ks = (k_h * jnp.exp2(dl)).astype(BF16)
                else:
                    qs, ks = q_ref[rows, cs], k_lat[rows, cs]
                attn = jnp.where(local, lax.dot_general(qs, ks, _NT, preferred_element_type=F32), 0.0)
                for n, h in enumerate(halves):
                    z = jnp.exp2(jnp.abs(b_h - _pivot_rows(b_h, h, h - 1 + d)) * neg_log2e)
                    qk = lax.dot_general((q_h * z).astype(BF16), (k_h * z).astype(BF16), _NT,
                                         preferred_element_type=F32)
                    attn = jnp.where(pair[n], qk, attn)
                attn_ref[i] = attn

        def emit(c, attn_ref, d=d, k_lat=k_lat, update_state=update_state):
            rows = pl.ds(pl.multiple_of(c * L, L), L)
            for i in range(hps):
                cs = slice(i * H_DIM, (i + 1) * H_DIM)
                qc = (q_ref[rows, cs].astype(F32) * jnp.exp(b_ref[rows, cs])).astype(BF16)
                o = (lax.dot_general(qc, st_ref[i].astype(BF16), _NT, preferred_element_type=F32)
                     + _dot(attn_ref[i].astype(BF16), v_ref[rows, cs]))
                if d == 0:
                    hacc_ref[rows, cs] = o
                else:
                    hs = hacc_ref[rows, cs] + o
                    y = _rms(hs, gain_ref[:, cs]) * sg_ref[rows, cs].astype(F32)
                    out_ref[rows, cs] = y.astype(BF16)
                update_state(i, rows, cs, k_lat, v_ref, b_ref)

        order = (lambda n: n) if d == 0 else (lambda n: nt - 1 - n)

        def lat_body(n, carry, pair_weights=pair_weights, emit=emit, order=order):
            c0, c1 = 2 * n, 2 * n + 1
            c2 = jnp.minimum(2 * n + 2, nt - 1)
            pair_weights(order(c1), attn1_ref)
            emit(order(c0), attn0_ref)
            pair_weights(order(c2), attn0_ref)
            emit(order(c1), attn1_ref)
            return carry

        pair_weights(order(0), attn0_ref)
        lax.fori_loop(0, nt // 2, lat_body, 0)


def _hgrn(beta, q, kf, kb, v, lff, lfb, sg, kfc, kbc, vc, lffc, lfbc, gain):
    B, T, W = q.shape
    Tc = vc.shape[1]
    hps = HEADS_PER_STEP
    G = H_HEADS // hps
    w = hps * H_DIM
    lat = pl.BlockSpec((None, T, w), lambda b, g: (b, 0, g))
    cx = pl.BlockSpec((None, Tc, w), lambda b, g: (b, 0, g))
    return pl.pallas_call(
        functools.partial(_hgrn_kernel, beta=beta),
        out_shape=jax.ShapeDtypeStruct((B, T, W), BF16),
        grid=(B, G),
        in_specs=[lat] * 7 + [cx] * 5 + [pl.BlockSpec((1, w), lambda b, g: (0, g))],
        out_specs=lat,
        scratch_shapes=[pltpu.VMEM((T, w), F32),
                        pltpu.VMEM((hps, H_DIM, H_DIM), F32),
                        pltpu.VMEM((T, w), F32),
                        pltpu.VMEM((Tc, w), F32),
                        pltpu.VMEM((hps, H_CHUNK, H_CHUNK), F32),
                        pltpu.VMEM((hps, H_CHUNK, H_CHUNK), F32)],
        compiler_params=_params(("parallel", "parallel")),
        name="hgrn_scan_b%d" % beta,
    )(q, kf, kb, v, lff, lfb, sg, kfc, kbc, vc, lffc, lfbc, gain)


def _merge_kernel(hm_ref, hh_ref, gm_ref, gh_ref, x_ref, mod_ref, pm_ref, ph_ref, wo_ref,
                  out_ref, *, gate_row):
    a = _dot(hm_ref[...], pm_ref[...])
    b = _dot(hh_ref[...], ph_ref[...])
    z = (gm_ref[...].astype(F32) * a + gh_ref[...].astype(F32) * b).astype(BF16)
    y = _dot(z, wo_ref[...])
    gate = mod_ref[0][gate_row:gate_row + 1]
    out_ref[...] = x_ref[...] + gate * y


def _merge(hm, hh, gm, gh, x2d, mod, tiles_per_mod, pm, ph, wo):
    N, D = x2d.shape
    tm = TOKEN_TILE
    tile = pl.BlockSpec((tm, D), lambda i: (i, 0))
    return pl.pallas_call(
        functools.partial(_merge_kernel, gate_row=5),
        out_shape=jax.ShapeDtypeStruct((N, D), F32),
        grid=(N // tm,),
        in_specs=[tile, tile, tile, tile, tile,
                  pl.BlockSpec((1, N_MOD, D), lambda i: (i // tiles_per_mod, 0, 0)),
                  _resident(pm.shape), _resident(ph.shape), _resident(wo.shape)],
        out_specs=tile,
        compiler_params=_params(("parallel",)),
        name="merge",
    )(hm, hh, gm, gh, x2d, mod, pm, ph, wo)


def _interleave_ffn_in(w):
    D, F2 = w.shape
    F = F2 // 2
    a = w[:, :F].reshape(D, F // FFN_CHUNK, 1, FFN_CHUNK)
    u = w[:, F:].reshape(D, F // FFN_CHUNK, 1, FFN_CHUNK)
    return jnp.concatenate([a, u], axis=2).reshape(D, F2)


def _to_col_major(a):
    B, T, C = a.shape
    return a.reshape(B, T // GRID_W, GRID_W, C).transpose(0, 2, 1, 3).reshape(B, T, C)


def _to_row_major(a):
    B, T, C = a.shape
    return a.reshape(B, GRID_W, T // GRID_W, C).transpose(0, 2, 1, 3).reshape(B, T, C)


def kernel(x, c, ctx, c_ctx, ada_w, ada_b, ffn1_norm, ffn1_w_in, ffn1_w_out, mix_norm,
           mix_w_in, mix_b_in, mlstm_norm, hgrn_lb_logits, hgrn_norm, proj_m, proj_h,
           mix_w_out, ffn2_norm, ffn2_w_in, ffn2_w_out, final_norm):
    B, T, D = x.shape
    Tc = ctx.shape[1]
    hps = HEADS_PER_STEP
    MQ, MV, HW = M_HEADS * M_QK_DIM, M_HEADS * M_V_DIM, H_HEADS * H_DIM

    pad = (-(B + 1)) % 8
    cc = jnp.concatenate([c, c_ctx[None, :], jnp.zeros((pad, D), F32)], axis=0)
    mod = _modulation(cc, ada_w[0], ada_b[0][None, :])
    ml = mod[:B].reshape(B, N_MOD, D)
    mc = mod[B:B + 1].reshape(1, N_MOD, D)

    row = lambda v: v.reshape(1, -1).astype(F32)
    w1_in = _interleave_ffn_in(ffn1_w_in[0]).astype(BF16)
    w1_out = ffn1_w_out[0].astype(BF16)
    w2_in = _interleave_ffn_in(ffn2_w_in[0]).astype(BF16)
    w2_out = ffn2_w_out[0].astype(BF16)

    W, bias = mix_w_in[0], mix_b_in[0]
    o = 0
    seg = {}
    for name, wd in (("mq", MQ), ("mk", MQ), ("mv", MV), ("mo", MV), ("ig", 2 * M_HEADS),
                     ("fg", 2 * M_HEADS), ("hq", HW), ("hff", HW), ("hfb", HW), ("hi", HW),
                     ("hg", HW), ("gm", D), ("gh", D)):
        seg[name] = (o, o + wd)
        o += wd
    cols = lambda name: jnp.arange(*seg[name])
    gate_cols = []
    for g in range(M_HEADS // hps):
        heads = jnp.arange(g * hps, (g + 1) * hps)
        gate_cols.append(jnp.concatenate([
            seg["ig"][0] + heads, seg["ig"][0] + M_HEADS + heads,
            seg["fg"][0] + heads, seg["fg"][0] + M_HEADS + heads]))

    def gather(parts, scale_q=False):
        ws, bs = [], []
        for p in parts:
            if p == "gates":
                for gc in gate_cols:
                    ws += [W[:, gc], jnp.zeros((D, LANES - 4 * hps), F32)]
                    bs += [bias[gc], jnp.zeros((LANES - 4 * hps,), F32)]
            else:
                s = M_QK_DIM ** -0.5 if p == "mq" else 1.0
                ws.append(W[:, seg[p][0]:seg[p][1]] * s)
                bs.append(bias[seg[p][0]:seg[p][1]] * s)
        return jnp.concatenate(ws, axis=1).astype(BF16), jnp.concatenate(bs)[None, :]

    n_gate = LANES * (M_HEADS // hps)
    lb = jnp.cumsum(jax.nn.softmax(hgrn_lb_logits.astype(F32), axis=1), axis=1)[:, 0]

    x2d = x.reshape(B * T, D)
    c2d = ctx.reshape(B * Tc, D)
    tpb = T // TOKEN_TILE
    (xm_c,) = _ffn(c2d, mc, None, row(ffn1_norm[0]), w1_in, w1_out, row(mix_norm[0]),
                   rows=(0, 1, 2), rows2=(3, 4), out_x=False, aux="mix")
    x1, xm_l = _ffn(x2d, ml, tpb, row(ffn1_norm[0]), w1_in, w1_out, row(mix_norm[0]),
                    rows=(0, 1, 2), rows2=(3, 4), out_x=True, aux="mix")

    wm_l, bm_l = gather(["mq", "mk", "mv", "mo", "gates", "gm", "gh"])
    mq, mk, mv, mso, mg, sgm, sgh = _proj(
        xm_l, wm_l, bm_l, lb,
        [(MQ, "bf16", 0), (MQ, "bf16", 0), (MV, "bf16", 0), (MV, "sigmoid", 0),
         (n_gate, "f32", 0), (D, "sigmoid", 0), (D, "sigmoid", 0)], "proj_mlstm")
    wm_c, bm_c = gather(["mk", "mv", "gates"])
    mkc, mvc, mgc = _proj(xm_c, wm_c, bm_c, lb,
                          [(MQ, "bf16", 0), (MV, "bf16", 0), (n_gate, "f32", 0)], "proj_mlstm_ctx")

    xm_cm = _to_col_major(xm_l.reshape(B, T, D)).reshape(B * T, D)
    wh_l, bh_l = gather(["hq", "hff", "hfb", "hi", "hg"])
    hq, lff, hkf, lfb, hkb, hv, hsg = _proj(
        xm_cm, wh_l, bh_l, lb,
        [(HW, "silu", 0), (HW, "hgrn", 0), (HW, "hgrn", 1), (HW, "bf16", 0), (HW, "silu", 0)],
        "proj_hgrn")
    wh_c, bh_c = gather(["hff", "hfb", "hi"])
    lffc, hkfc, lfbc, hkbc, hvc = _proj(
        xm_c, wh_c, bh_c, lb, [(HW, "hgrn", 0), (HW, "hgrn", 1), (HW, "bf16", 0)], "proj_hgrn_ctx")

    r3 = lambda a, t: a.reshape(B, t, a.shape[-1])
    hm = _mlstm(r3(mq, T), r3(mk, T), r3(mv, T), r3(mg, T), r3(mso, T),
                r3(mkc, Tc), r3(mvc, Tc), r3(mgc, Tc), row(mlstm_norm[0]))
    hgrn_args = (r3(hq, T), r3(hkf, T), r3(hkb, T), r3(hv, T), r3(lff, T), r3(lfb, T), r3(hsg, T),
                 r3(hkfc, Tc), r3(hkbc, Tc), r3(hvc, Tc), r3(lffc, Tc), r3(lfbc, Tc),
                 row(hgrn_norm[0]))
    block_ok = H_BLOCK * jnp.max(-jnp.log(lb)) <= H_BLOCK_MAX_EXPONENT
    hh_cm = lax.cond(block_ok, functools.partial(_hgrn, H_BLOCK), functools.partial(_hgrn, 1),
                     *hgrn_args)
    hh = _to_row_major(hh_cm)

    x2 = _merge(hm.reshape(B * T, MV), hh.reshape(B * T, HW), sgm, sgh, x1, ml, tpb,
                proj_m[0].astype(BF16), proj_h[0].astype(BF16), mix_w_out[0].astype(BF16))
    (out,) = _ffn(x2, ml, tpb, row(ffn2_norm[0]), w2_in, w2_out, row(final_norm),
                  rows=(6, 7, 8), rows2=(3, 4), out_x=False, aux="final")
    return out.reshape(B, T, D)
```

```python
import functools

import jax
import jax.numpy as jnp
from jax import lax
from jax.experimental import pallas as pl
from jax.experimental.pallas import tpu as pltpu

F32 = jnp.float32
BF16 = jnp.bfloat16
EPS = 1e-6

GRID_W = 64
M_HEADS = 8
M_QK_DIM = 64
M_V_DIM = 128
H_HEADS = 8
H_DIM = 128
N_MOD = 9

LANES = 128
TOKEN_TILE = 512
FFN_CHUNK = 256
PROJ_CHUNK = 512
M_CHUNK = 128
H_CHUNK = 64
H_BLOCK = 16
H_BLOCK_MAX_EXPONENT = 64.0
HEADS_PER_STEP = 4
VMEM_LIMIT = 56 * 1024 * 1024

_NT = (((1,), (1,)), ((), ()))


def _params(sem, vmem=VMEM_LIMIT):
    return pltpu.CompilerParams(dimension_semantics=sem, vmem_limit_bytes=vmem)


def _resident(shape):
    nd = len(shape)
    return pl.BlockSpec(shape, lambda *_: (0,) * nd, pipeline_mode=pl.Buffered(1))


def _dot(a, b):
    return jnp.dot(a, b, preferred_element_type=F32)


def _rms(x, g):
    ms = jnp.mean(x * x, axis=-1, keepdims=True)
    return x * lax.rsqrt(ms + EPS) * g


def _split3(x):
    hi = x.astype(BF16)
    r = x - hi.astype(F32)
    mid = r.astype(BF16)
    lo = (r - mid.astype(F32)).astype(BF16)
    return hi, mid, lo


def _tri_cumsum(tri, x):
    hi, mid, _ = _split3(x)
    return _dot(tri, hi) + _dot(tri, mid)


def _mod_kernel(c_ref, w_ref, b_ref, o_ref):
    c = c_ref[...]
    a = c * jax.nn.sigmoid(c)
    a_hi = a.astype(BF16)
    a_lo = (a - a_hi.astype(F32)).astype(BF16)
    w = w_ref[...]
    w_hi = w.astype(BF16)
    w_lo = (w - w_hi.astype(F32)).astype(BF16)
    o_ref[...] = _dot(a_hi, w_hi) + _dot(a_hi, w_lo) + _dot(a_lo, w_hi) + b_ref[...]


def _modulation(cc, w, b):
    R, D = cc.shape
    N = w.shape[1]
    tn = 1024
    return pl.pallas_call(
        _mod_kernel,
        out_shape=jax.ShapeDtypeStruct((R, N), F32),
        grid=(N // tn,),
        in_specs=[pl.BlockSpec((R, D), lambda j: (0, 0)),
                  pl.BlockSpec((D, tn), lambda j: (0, j)),
                  pl.BlockSpec((1, tn), lambda j: (0, j))],
        out_specs=pl.BlockSpec((R, tn), lambda j: (0, j)),
        compiler_params=_params(("arbitrary",)),
        name="modulation",
    )(cc, w, b)


def _ffn_kernel(x_ref, mod_ref, g_ref, win_ref, wout_ref, g2_ref, *rest,
                rows, rows2, out_x, aux, n_chunks, merge_gate_row):
    fc = FFN_CHUNK
    x = x_ref[...]
    mod = mod_ref[0]
    outs = rest
    if merge_gate_row is not None:
        hm_ref, hh_ref, gm_ref, gh_ref, pm_ref, ph_ref, wo_ref = rest[:7]
        outs = rest[7:]
        z = (gm_ref[...].astype(F32) * _dot(hm_ref[...], pm_ref[...])
             + gh_ref[...].astype(F32) * _dot(hh_ref[...], ph_ref[...])).astype(BF16)
        x = x + mod[merge_gate_row:merge_gate_row + 1] * _dot(z, wo_ref[...])
    shift = mod[rows[0]:rows[0] + 1]
    scale = mod[rows[1]:rows[1] + 1]
    gate = mod[rows[2]:rows[2] + 1]
    xm = (_rms(x, g_ref[...]) * (1.0 + scale) + shift).astype(BF16)
    acc = jnp.zeros(x.shape, F32)
    for j in range(n_chunks):
        au = _dot(xm, win_ref[:, j * 2 * fc:(j + 1) * 2 * fc])
        a = au[:, :fc]
        u = au[:, fc:]
        h = (a * jax.nn.sigmoid(a) * u).astype(BF16)
        acc = acc + _dot(h, wout_ref[j * fc:(j + 1) * fc, :])
    xn = x + 0.5 * gate * acc
    k = 0
    if out_x:
        outs[k][...] = xn
        k += 1
    if aux == "mix":
        shift2 = mod[rows2[0]:rows2[0] + 1]
        scale2 = mod[rows2[1]:rows2[1] + 1]
        outs[k][...] = (_rms(xn, g2_ref[...]) * (1.0 + scale2) + shift2).astype(BF16)
    elif aux == "final":
        outs[k][...] = _rms(xn, g2_ref[...])


def _ffn(x2d, mod, tiles_per_mod, g, win, wout, g2, *, rows, rows2, out_x, aux, merge=None):
    N, D = x2d.shape
    F = wout.shape[0]
    tm = TOKEN_TILE
    out_shape, out_specs = [], []
    if out_x:
        out_shape.append(jax.ShapeDtypeStruct((N, D), F32))
        out_specs.append(pl.BlockSpec((tm, D), lambda i: (i, 0)))
    if aux == "mix":
        out_shape.append(jax.ShapeDtypeStruct((N, D), BF16))
        out_specs.append(pl.BlockSpec((tm, D), lambda i: (i, 0)))
    elif aux == "final":
        out_shape.append(jax.ShapeDtypeStruct((N, D), F32))
        out_specs.append(pl.BlockSpec((tm, D), lambda i: (i, 0)))
    if tiles_per_mod is None:
        mod_map = lambda i: (0, 0, 0)
    else:
        mod_map = lambda i: (i // tiles_per_mod, 0, 0)
    kern = functools.partial(_ffn_kernel, rows=rows, rows2=rows2, out_x=out_x, aux=aux,
                             n_chunks=F // FFN_CHUNK,
                             merge_gate_row=None if merge is None else merge[7])
    tile = pl.BlockSpec((tm, D), lambda i: (i, 0))
    in_specs = [tile, pl.BlockSpec((1, N_MOD, D), mod_map), _resident((1, D)),
                _resident(win.shape), _resident(wout.shape), _resident((1, D))]
    args = [x2d, mod, g, win, wout, g2]
    if merge is not None:
        in_specs += [tile] * 4 + [_resident(w.shape) for w in merge[4:7]]
        args += list(merge[:7])
    return pl.pallas_call(
        kern,
        out_shape=out_shape,
        grid=(N // tm,),
        in_specs=in_specs,
        out_specs=out_specs,
        compiler_params=_params(("parallel",)),
        name=("merge_" if merge is not None else "") + "ffn_" + aux,
    )(*args)


def _proj_kernel(x_ref, w_ref, b_ref, lb_ref, *outs, segs):
    x = x_ref[...]
    c0 = 0
    oi = 0
    for wd, kind, arg in segs:
        for s in range(0, wd, PROJ_CHUNK):
            w = min(PROJ_CHUNK, wd - s)
            y = _dot(x, w_ref[:, c0 + s:c0 + s + w]) + b_ref[:, c0 + s:c0 + s + w]
            if kind == "bf16":
                outs[oi][:, s:s + w] = y.astype(BF16)
            elif kind == "f32":
                outs[oi][:, s:s + w] = y
            elif kind == "sigmoid":
                outs[oi][:, s:s + w] = jax.nn.sigmoid(y).astype(BF16)
            elif kind == "silu":
                outs[oi][:, s:s + w] = (y * jax.nn.sigmoid(y)).astype(BF16)
            elif kind == "hgrn":
                lb = lb_ref[arg:arg + 1, s:s + w]
                sg = jax.nn.sigmoid(y)
                outs[oi][:, s:s + w] = jnp.log(lb + (1.0 - lb) * sg)
                outs[oi + 1][:, s:s + w] = ((1.0 - lb) * (1.0 - sg)).astype(BF16)
        c0 += wd
        oi += 2 if kind == "hgrn" else 1


def _proj(x2d, w, b, lb, segs, name):
    N, D = x2d.shape
    tm = TOKEN_TILE
    out_shape, out_specs = [], []
    for wd, kind, _ in segs:
        dts = {"bf16": [BF16], "f32": [F32], "sigmoid": [BF16], "silu": [BF16],
               "hgrn": [F32, BF16]}[kind]
        for dt in dts:
            out_shape.append(jax.ShapeDtypeStruct((N, wd), dt))
            out_specs.append(pl.BlockSpec((tm, wd), lambda i: (i, 0)))
    return pl.pallas_call(
        functools.partial(_proj_kernel, segs=tuple(segs)),
        out_shape=out_shape,
        grid=(N // tm,),
        in_specs=[pl.BlockSpec((tm, D), lambda i: (i, 0)),
                  _resident(w.shape),
                  _resident(b.shape),
                  _resident(lb.shape)],
        out_specs=out_specs,
        compiler_params=_params(("parallel",)),
        name=name,
    )(x2d, w, b, lb)


def _mlstm_kernel(q_ref, k_ref, kt_ref, v_ref, g_ref, so_ref, ktc_ref, vc_ref, gc_ref, gain_ref,
                  out_ref, hacc_ref, c_ref, m_ref, rows_ref,
                  s0_ref, wi0_ref, fl0_ref, s1_ref, wi1_ref, fl1_ref):
    L = M_CHUNK
    hps = HEADS_PER_STEP
    T = q_ref.shape[0]
    Tc = vc_ref.shape[0]
    nt, nct = T // L, Tc // L
    assert nt % 2 == 0
    row = lax.broadcasted_iota(jnp.int32, (L, L), 0)
    col = lax.broadcasted_iota(jnp.int32, (L, L), 1)
    lane = lax.broadcasted_iota(jnp.int32, (L, LANES), 1)
    krow = lax.broadcasted_iota(jnp.int32, (LANES, 1), 0)
    ones_ext = jnp.ones((L, LANES), BF16)
    R_R, R_LF, R_W, R_M, R_DEC = (k * hps for k in range(5))

    for d in (0, 1):
        tri = (col <= row) if d == 0 else (col >= row)
        trif_t = ((col >= row) if d == 0 else (col <= row)).astype(F32)
        c_ref[...] = jnp.zeros(c_ref.shape, F32)
        m_ref[...] = jnp.zeros(m_ref.shape, F32)

        def gate_body(n, carry, gref, slot0, d=d, trif_t=trif_t):
            GT = gref[pl.ds(pl.multiple_of(n * L, L), L), :].T[:4 * hps, :]
            lf = jax.nn.log_sigmoid(GT[(2 + d) * hps:(3 + d) * hps, :])
            b = sum(_dot(part.astype(F32), trif_t) for part in _split3(lf))
            r = GT[d * hps:(d + 1) * hps, :] - b
            b_last = b[:, L - 1:L] if d == 0 else b[:, 0:1]
            wlog = b_last + r
            wide = lambda x: jnp.broadcast_to(x, (hps, L))
            rows_ref[slot0 + n] = jnp.concatenate(
                [r, lf, wlog, wide(b_last), wide(jnp.max(wlog, axis=1, keepdims=True)),
                 jnp.zeros((rows_ref.shape[1] - 5 * hps, L), F32)], axis=0)
            return carry

        def stab_body(n, carry, nchunks, slot0, d=d):
            idx = slot0 + (n if d == 0 else nchunks - 1 - n)
            R = rows_ref[idx]
            m_prev = m_ref[0:hps, 0:1]
            bm = R[R_M:R_M + hps, 0:1] + m_prev
            m_new = jnp.maximum(bm, R[R_DEC:R_DEC + hps, 0:1])
            wide = lambda x: jnp.broadcast_to(x, (hps, L))
            rows_ref[idx, R_W:R_W + 3 * hps, :] = jnp.concatenate(
                [jnp.exp(R[R_W:R_W + hps, :] - m_new), wide(m_prev), wide(jnp.exp(bm - m_new))], axis=0)
            m_ref[0:hps, :] = jnp.broadcast_to(m_new, (hps, LANES))
            return carry

        lax.fori_loop(0, nct, functools.partial(gate_body, gref=gc_ref, slot0=0), 0, unroll=True)
        lax.fori_loop(0, nt, functools.partial(gate_body, gref=g_ref, slot0=nct), 0, unroll=4)
        lax.fori_loop(0, nct, functools.partial(stab_body, nchunks=nct, slot0=0), 0)
        lax.fori_loop(0, nt, functools.partial(stab_body, nchunks=nt, slot0=nct), 0)

        def update_state(p, R, rows, ktref, vref):
            kT = ktref[p * LANES:(p + 1) * LANES, rows].astype(F32)
            upd = None
            decay_rows = None
            for a in range(2):
                i = 2 * p + a
                vext = jnp.concatenate([vref[rows, i * LANES:(i + 1) * LANES], ones_ext], axis=1)
                rowmask = (krow < M_QK_DIM) if a == 0 else (krow >= M_QK_DIM)
                kw = jnp.where(rowmask, kT * R[R_W + i:R_W + i + 1, :], 0.0)
                u = _dot(kw.astype(BF16).astype(F32), vext.astype(F32))
                upd = u if upd is None else upd + u
                dr = jnp.where(rowmask, R[R_DEC + i:R_DEC + i + 1, 0:1], 0.0)
                decay_rows = dr if decay_rows is None else decay_rows + dr
            c_ref[p] = decay_rows * c_ref[p] + upd

        def ctx_body(n, carry, d=d, update_state=update_state):
            c = n if d == 0 else nct - 1 - n
            rows = pl.ds(pl.multiple_of(c * L, L), L)
            for p in range(hps // 2):
                update_state(p, rows_ref[c], rows, ktc_ref, vc_ref)
            return carry

        lax.fori_loop(0, nct, ctx_body, 0)

        def head_q(rows, p, a):
            q2 = q_ref[rows, p * LANES:(p + 1) * LANES]
            headmask = (lane < M_QK_DIM) if a == 0 else (lane >= M_QK_DIM)
            return jnp.where(headmask, q2, jnp.zeros_like(q2))

        def pair_weights(c, s_ref, wi_ref, fl_ref, tri=tri, head_q=head_q):
            rows = pl.ds(pl.multiple_of(c * L, L), L)
            R = rows_ref[nct + c]
            for i in range(hps):
                p, a = divmod(i, 2)
                s_raw = lax.dot_general(head_q(rows, p, a), k_ref[rows, p * LANES:(p + 1) * LANES],
                                        _NT, preferred_element_type=F32)
                m_prev = R[R_M + i:R_M + i + 1, 0:1]
                rm = jnp.where(tri, R[R_R + i:R_R + i + 1, :], -jnp.inf)
                mu = jnp.maximum(m_prev, jnp.max(rm, axis=1, keepdims=True))
                bcol = jnp.sum(jnp.where(tri, R[R_LF + i:R_LF + i + 1, :], 0.0), axis=1, keepdims=True)
                mu_b = jnp.broadcast_to(mu, (L, L))
                s_ref[i] = s_raw * jnp.exp(rm - mu_b)
                wi_ref[i] = jnp.exp(m_prev - mu_b)
                fl_ref[i] = jnp.exp(-jnp.broadcast_to(bcol + mu, (L, LANES)))

        def emit(c, s_ref, wi_ref, fl_ref, d=d, head_q=head_q, update_state=update_state):
            rows = pl.ds(pl.multiple_of(c * L, L), L)
            for p in range(hps // 2):
                C2b = c_ref[p].astype(BF16)
                for a in range(2):
                    i = 2 * p + a
                    cs = slice(i * LANES, (i + 1) * LANES)
                    vext = jnp.concatenate([v_ref[rows, cs], ones_ext], axis=1)
                    nd = (jnp.concatenate([wi_ref[i]] * 2, axis=1) * _dot(head_q(rows, p, a), C2b)
                          + _dot(s_ref[i].astype(BF16), vext))
                    h = nd[:, :LANES] / jnp.maximum(jnp.abs(nd[:, LANES:]), fl_ref[i])
                    if d == 0:
                        hacc_ref[rows, cs] = h
                    else:
                        hs = hacc_ref[rows, cs] + h
                        y = _rms(hs, gain_ref[:, cs]) * so_ref[rows, cs].astype(F32)
                        out_ref[rows, cs] = y.astype(BF16)
                update_state(p, rows_ref[nct + c], rows, kt_ref, v_ref)

        order = (lambda n: n) if d == 0 else (lambda n: nt - 1 - n)
        slot0 = (s0_ref, wi0_ref, fl0_ref)
        slot1 = (s1_ref, wi1_ref, fl1_ref)

        def lat_body(n, carry, pair_weights=pair_weights, emit=emit, order=order):
            c0, c1 = 2 * n, 2 * n + 1
            c2 = jnp.minimum(2 * n + 2, nt - 1)
            pair_weights(order(c1), *slot1)
            emit(order(c0), *slot0)
            pair_weights(order(c2), *slot0)
            emit(order(c1), *slot1)
            return carry

        pair_weights(order(0), *slot0)
        lax.fori_loop(0, nt // 2, lat_body, 0)


def _mlstm(q, k, kt, v, g, so, ktc, vc, gc, gain):
    B, T, _ = q.shape
    Tc = vc.shape[1]
    hps = HEADS_PER_STEP
    G = M_HEADS // hps
    wq, wv = hps * M_QK_DIM, hps * M_V_DIM
    lat = lambda w: pl.BlockSpec((None, T, w), lambda b, g: (b, 0, g))
    cx = lambda w: pl.BlockSpec((None, Tc, w), lambda b, g: (b, 0, g))
    return pl.pallas_call(
        _mlstm_kernel,
        out_shape=jax.ShapeDtypeStruct((B, T, M_HEADS * M_V_DIM), BF16),
        grid=(B, G),
        in_specs=[lat(wq), lat(wq), pl.BlockSpec((None, wq, T), lambda b, g: (b, g, 0)),
                  lat(wv), lat(LANES), lat(wv),
                  pl.BlockSpec((None, wq, Tc), lambda b, g: (b, g, 0)), cx(wv), cx(LANES),
                  pl.BlockSpec((1, wv), lambda b, g: (0, g))],
        out_specs=lat(wv),
        scratch_shapes=[pltpu.VMEM((T, wv), F32),
                        pltpu.VMEM((hps // 2, 2 * M_QK_DIM, 2 * LANES), F32),
                        pltpu.VMEM((8, LANES), F32),
                        pltpu.VMEM(((T + Tc) // M_CHUNK, 8 * hps, M_CHUNK), F32)]
                       + [pltpu.VMEM((hps, M_CHUNK, M_CHUNK), F32)] * 6,
        compiler_params=_params(("parallel", "parallel")),
        name="mlstm_scan",
    )(q, k, kt, v, g, so, ktc, vc, gc, gain)


def _pivot_rows(b, h, off):
    L, N = b.shape
    bc = lambda r, n: jnp.broadcast_to(b[r:r + 1, :], (n, N))
    if 2 * h >= 8:
        return jnp.concatenate([bc(x + off, 2 * h) for x in range(0, L, 2 * h)], axis=0)
    sub = lax.broadcasted_iota(jnp.int32, (8, N), 0)
    tiles = []
    for x in range(0, L, 8):
        t = bc(x + 8 - 2 * h + off, 8)
        for y in range(8 - 4 * h, -1, -2 * h):
            t = jnp.where(sub < y + 2 * h, bc(x + y + off, 8), t)
        tiles.append(t)
    return jnp.concatenate(tiles, axis=0)


def _block_entry_rows(b, beta, d):
    L, N = b.shape
    tiles = []
    for x in range(0, L, beta):
        r = x - 1 if d == 0 else x + beta
        inside = 0 <= r < L
        tiles.append(jnp.broadcast_to(b[r:r + 1, :], (beta, N)) if inside else jnp.zeros((beta, N), F32))
    return jnp.concatenate(tiles, axis=0)


def _hgrn_kernel(q_ref, kf_ref, kb_ref, v_ref, lff_ref, lfb_ref, sg_ref,
                 kfc_ref, kbc_ref, vc_ref, lffc_ref, lfbc_ref, gain_ref,
                 out_ref, hacc_ref, st_ref, b_ref, bc_ref, attn0_ref, attn1_ref, *, beta):
    L = H_CHUNK
    hps = HEADS_PER_STEP
    T = q_ref.shape[0]
    Tc = vc_ref.shape[0]
    nt, nct = T // L, Tc // L
    assert nt % 2 == 0
    row = lax.broadcasted_iota(jnp.int32, (L, L), 0)
    col = lax.broadcasted_iota(jnp.int32, (L, L), 1)
    halves = [h for h in (L >> (n + 1) for n in range(L.bit_length() - 1)) if h >= beta]
    zpad_f = jnp.zeros((H_DIM - L, H_DIM), F32)
    zpad_b = jnp.zeros((H_DIM - L, H_DIM), BF16)
    neg_log2e = -1.4426950408889634

    for d in (0, 1):
        tri = (col <= row) if d == 0 else (col >= row)
        trib = tri.astype(BF16)
        last = L - 1 if d == 0 else 0
        st_ref[...] = jnp.zeros(st_ref.shape, F32)
        k_lat, lf_lat = (kf_ref, lff_ref) if d == 0 else (kb_ref, lfb_ref)
        k_ctx, lf_ctx = (kfc_ref, lffc_ref) if d == 0 else (kbc_ref, lfbc_ref)
        local = tri & ((row & -beta) == (col & -beta))
        pair = []
        for h in halves:
            same = (row & -(2 * h)) == (col & -(2 * h))
            r_late = (row & (2 * h - 1)) >= h
            c_late = (col & (2 * h - 1)) >= h
            pair.append(same & (r_late & ~c_late if d == 0 else ~r_late & c_late))

        def cumsum_body(n, carry, lfref, bref, trib=trib):
            rows = pl.ds(pl.multiple_of(n * L, L), L)
            bref[rows, :] = _tri_cumsum(trib, lfref[rows, :])
            return carry

        lax.fori_loop(0, nct, functools.partial(cumsum_body, lfref=lf_ctx, bref=bc_ref), 0)
        lax.fori_loop(0, nt, functools.partial(cumsum_body, lfref=lf_lat, bref=b_ref), 0, unroll=2)

        def update_state(i, rows, cs, kref, vref, bref):
            b_h = bref[rows, cs]
            bl = b_h[last:last + 1, :]
            kd = (kref[rows, cs].astype(F32) * jnp.exp(bl - b_h)).astype(BF16)
            vT = jnp.concatenate([vref[rows, cs].astype(F32), zpad_f], axis=0).T.astype(BF16)
            st_ref[i] = (st_ref[i] * jnp.exp(bl)
                         + _dot(vT, jnp.concatenate([kd, zpad_b], axis=0)))

        def ctx_body(n, carry, d=d, k_ctx=k_ctx, update_state=update_state):
            c = n if d == 0 else nct - 1 - n
            rows = pl.ds(pl.multiple_of(c * L, L), L)
            for i in range(hps):
                update_state(i, rows, slice(i * H_DIM, (i + 1) * H_DIM), k_ctx, vc_ref, bc_ref)
            return carry

        lax.fori_loop(0, nct, ctx_body, 0)

        def pair_weights(c, attn_ref, d=d, k_lat=k_lat, local=local, pair=pair):
            rows = pl.ds(pl.multiple_of(c * L, L), L)
            for i in range(hps):
                cs = slice(i * H_DIM, (i + 1) * H_DIM)
                b_h = b_ref[rows, cs]
                q_h = q_ref[rows, cs].astype(F32)
                k_h = k_lat[rows, cs].astype(F32)
                if beta > 1:
                    dl = (b_h - _block_entry_rows(b_h, beta, d)) * neg_log2e
                    qs = (q_h * jnp.exp2(-dl)).astype(BF16)
                    ks = (k_h * jnp.exp2(dl)).astype(BF16)
                else:
                    qs, ks = q_ref[rows, cs], k_lat[rows, cs]
                attn = jnp.where(local, lax.dot_general(qs, ks, _NT, preferred_element_type=F32), 0.0)
                for n, h in enumerate(halves):
                    z = jnp.exp2(jnp.abs(b_h - _pivot_rows(b_h, h, h - 1 + d)) * neg_log2e)
                    qk = lax.dot_general((q_h * z).astype(BF16), (k_h * z).astype(BF16), _NT,
                                         preferred_element_type=F32)
                    attn = jnp.where(pair[n], qk, attn)
                attn_ref[i] = attn

        def emit(c, attn_ref, d=d, k_lat=k_lat, update_state=update_state):
            rows = pl.ds(pl.multiple_of(c * L, L), L)
            for i in range(hps):
                cs = slice(i * H_DIM, (i + 1) * H_DIM)
                qc = (q_ref[rows, cs].astype(F32) * jnp.exp(b_ref[rows, cs])).astype(BF16)
                o = (lax.dot_general(qc, st_ref[i].astype(BF16), _NT, preferred_element_type=F32)
                     + _dot(attn_ref[i].astype(BF16), v_ref[rows, cs]))
                if d == 0:
                    hacc_ref[rows, cs] = o
                else:
                    hs = hacc_ref[rows, cs] + o
                    y = _rms(hs, gain_ref[:, cs]) * sg_ref[rows, cs].astype(F32)
                    out_ref[rows, cs] = y.astype(BF16)
                update_state(i, rows, cs, k_lat, v_ref, b_ref)

        order = (lambda n: n) if d == 0 else (lambda n: nt - 1 - n)

        def lat_body(n, carry, pair_weights=pair_weights, emit=emit, order=order):
            c0, c1 = 2 * n, 2 * n + 1
            c2 = jnp.minimum(2 * n + 2, nt - 1)
            pair_weights(order(c1), attn1_ref)
            emit(order(c0), attn0_ref)
            pair_weights(order(c2), attn0_ref)
            emit(order(c1), attn1_ref)
            return carry

        pair_weights(order(0), attn0_ref)
        lax.fori_loop(0, nt // 2, lat_body, 0)


def _hgrn(beta, q, kf, kb, v, lff, lfb, sg, kfc, kbc, vc, lffc, lfbc, gain):
    B, T, W = q.shape
    Tc = vc.shape[1]
    hps = HEADS_PER_STEP
    G = H_HEADS // hps
    w = hps * H_DIM
    lat = pl.BlockSpec((None, T, w), lambda b, g: (b, 0, g))
    cx = pl.BlockSpec((None, Tc, w), lambda b, g: (b, 0, g))
    return pl.pallas_call(
        functools.partial(_hgrn_kernel, beta=beta),
        out_shape=jax.ShapeDtypeStruct((B, T, W), BF16),
        grid=(B, G),
        in_specs=[lat] * 7 + [cx] * 5 + [pl.BlockSpec((1, w), lambda b, g: (0, g))],
        out_specs=lat,
        scratch_shapes=[pltpu.VMEM((T, w), F32),
                        pltpu.VMEM((hps, H_DIM, H_DIM), F32),
                        pltpu.VMEM((T, w), F32),
                        pltpu.VMEM((Tc, w), F32),
                        pltpu.VMEM((hps, H_CHUNK, H_CHUNK), F32),
                        pltpu.VMEM((hps, H_CHUNK, H_CHUNK), F32)],
        compiler_params=_params(("parallel", "parallel")),
        name="hgrn_scan_b%d" % beta,
    )(q, kf, kb, v, lff, lfb, sg, kfc, kbc, vc, lffc, lfbc, gain)


def _interleave_ffn_in(w):
    D, F2 = w.shape
    F = F2 // 2
    a = w[:, :F].reshape(D, F // FFN_CHUNK, 1, FFN_CHUNK)
    u = w[:, F:].reshape(D, F // FFN_CHUNK, 1, FFN_CHUNK)
    return jnp.concatenate([a, u], axis=2).reshape(D, F2)


def _to_col_major(a):
    B, T, C = a.shape
    return a.reshape(B, T // GRID_W, GRID_W, C).transpose(0, 2, 1, 3).reshape(B, T, C)


def _to_row_major(a):
    B, T, C = a.shape
    return a.reshape(B, GRID_W, T // GRID_W, C).transpose(0, 2, 1, 3).reshape(B, T, C)


def kernel(x, c, ctx, c_ctx, ada_w, ada_b, ffn1_norm, ffn1_w_in, ffn1_w_out, mix_norm,
           mix_w_in, mix_b_in, mlstm_norm, hgrn_lb_logits, hgrn_norm, proj_m, proj_h,
           mix_w_out, ffn2_norm, ffn2_w_in, ffn2_w_out, final_norm):
    B, T, D = x.shape
    Tc = ctx.shape[1]
    hps = HEADS_PER_STEP
    MQ, MV, HW = M_HEADS * M_QK_DIM, M_HEADS * M_V_DIM, H_HEADS * H_DIM

    pad = (-(B + 1)) % 8
    cc = jnp.concatenate([c, c_ctx[None, :], jnp.zeros((pad, D), F32)], axis=0)
    mod = _modulation(cc, ada_w[0], ada_b[0][None, :])
    ml = mod[:B].reshape(B, N_MOD, D)
    mc = mod[B:B + 1].reshape(1, N_MOD, D)

    row = lambda v: v.reshape(1, -1).astype(F32)
    w1_in = _interleave_ffn_in(ffn1_w_in[0]).astype(BF16)
    w1_out = ffn1_w_out[0].astype(BF16)
    w2_in = _interleave_ffn_in(ffn2_w_in[0]).astype(BF16)
    w2_out = ffn2_w_out[0].astype(BF16)

    W, bias = mix_w_in[0], mix_b_in[0]
    o = 0
    seg = {}
    for name, wd in (("mq", MQ), ("mk", MQ), ("mv", MV), ("mo", MV), ("ig", 2 * M_HEADS),
                     ("fg", 2 * M_HEADS), ("hq", HW), ("hff", HW), ("hfb", HW), ("hi", HW),
                     ("hg", HW), ("gm", D), ("gh", D)):
        seg[name] = (o, o + wd)
        o += wd
    gate_cols = []
    for g in range(M_HEADS // hps):
        gate_cols.append([(seg[n][0] + o + g * hps, seg[n][0] + o + (g + 1) * hps)
                          for n in ("ig", "fg") for o in (0, M_HEADS)])

    def gather(parts, scale_q=False):
        ws, bs = [], []
        for p in parts:
            if p == "gates":
                for ranges in gate_cols:
                    ws += [W[:, a:b] for a, b in ranges] + [jnp.zeros((D, LANES - 4 * hps), F32)]
                    bs += [bias[a:b] for a, b in ranges] + [jnp.zeros((LANES - 4 * hps,), F32)]
            else:
                s = M_QK_DIM ** -0.5 if p == "mq" else 1.0
                ws.append(W[:, seg[p][0]:seg[p][1]] * s)
                bs.append(bias[seg[p][0]:seg[p][1]] * s)
        return jnp.concatenate(ws, axis=1).astype(BF16), jnp.concatenate(bs)[None, :]

    n_gate = LANES * (M_HEADS // hps)
    lb = jnp.cumsum(jax.nn.softmax(hgrn_lb_logits.astype(F32), axis=1), axis=1)[:, 0]

    x2d = x.reshape(B * T, D)
    c2d = ctx.reshape(B * Tc, D)
    tpb = T // TOKEN_TILE
    (xm_c,) = _ffn(c2d, mc, None, row(ffn1_norm[0]), w1_in, w1_out, row(mix_norm[0]),
                   rows=(0, 1, 2), rows2=(3, 4), out_x=False, aux="mix")
    x1, xm_l = _ffn(x2d, ml, tpb, row(ffn1_norm[0]), w1_in, w1_out, row(mix_norm[0]),
                    rows=(0, 1, 2), rows2=(3, 4), out_x=True, aux="mix")

    wm_l, bm_l = gather(["mq", "mk", "mv", "mo", "gates", "gm", "gh"])
    mq, mk, mv, mso, mg, sgm, sgh = _proj(
        xm_l, wm_l, bm_l, lb,
        [(MQ, "bf16", 0), (MQ, "bf16", 0), (MV, "bf16", 0), (MV, "sigmoid", 0),
         (n_gate, "f32", 0), (D, "sigmoid", 0), (D, "sigmoid", 0)], "proj_mlstm")
    wm_c, bm_c = gather(["mk", "mv", "gates"])
    mkc, mvc, mgc = _proj(xm_c, wm_c, bm_c, lb,
                          [(MQ, "bf16", 0), (MV, "bf16", 0), (n_gate, "f32", 0)], "proj_mlstm_ctx")

    xm_cm = _to_col_major(xm_l.reshape(B, T, D)).reshape(B * T, D)
    wh_l, bh_l = gather(["hq", "hff", "hfb", "hi", "hg"])
    hq, lff, hkf, lfb, hkb, hv, hsg = _proj(
        xm_cm, wh_l, bh_l, lb,
        [(HW, "silu", 0), (HW, "hgrn", 0), (HW, "hgrn", 1), (HW, "bf16", 0), (HW, "silu", 0)],
        "proj_hgrn")
    wh_c, bh_c = gather(["hff", "hfb", "hi"])
    lffc, hkfc, lfbc, hkbc, hvc = _proj(
        xm_c, wh_c, bh_c, lb, [(HW, "hgrn", 0), (HW, "hgrn", 1), (HW, "bf16", 0)], "proj_hgrn_ctx")

    r3 = lambda a, t: a.reshape(B, t, a.shape[-1])
    tr = lambda a: jnp.swapaxes(a, 1, 2)
    hm = _mlstm(r3(mq, T), r3(mk, T), tr(r3(mk, T)), r3(mv, T), r3(mg, T), r3(mso, T),
                tr(r3(mkc, Tc)), r3(mvc, Tc), r3(mgc, Tc), row(mlstm_norm[0]))
    hgrn_args = (r3(hq, T), r3(hkf, T), r3(hkb, T), r3(hv, T), r3(lff, T), r3(lfb, T), r3(hsg, T),
                 r3(hkfc, Tc), r3(hkbc, Tc), r3(hvc, Tc), r3(lffc, Tc), r3(lfbc, Tc),
                 row(hgrn_norm[0]))
    block_ok = H_BLOCK * jnp.max(-jnp.log(lb)) <= H_BLOCK_MAX_EXPONENT
    hh_cm = lax.cond(block_ok, functools.partial(_hgrn, H_BLOCK), functools.partial(_hgrn, 1),
                     *hgrn_args)
    hh = _to_row_major(hh_cm)

    merge = (hm.reshape(B * T, MV), hh.reshape(B * T, HW), sgm, sgh,
             proj_m[0].astype(BF16), proj_h[0].astype(BF16), mix_w_out[0].astype(BF16), 5)
    (out,) = _ffn(x1, ml, tpb, row(ffn2_norm[0]), w2_in, w2_out, row(final_norm),
                  rows=(6, 7, 8), rows2=(3, 4), out_x=False, aux="final", merge=merge)
    return out.reshape(B, T, D)
```

```python
import functools

import jax
import jax.numpy as jnp
from jax import lax
from jax.experimental import pallas as pl
from jax.experimental.pallas import tpu as pltpu

F32 = jnp.float32
BF16 = jnp.bfloat16
EPS = 1e-6

GRID_W = 64
M_HEADS = 8
M_QK_DIM = 64
M_V_DIM = 128
H_HEADS = 8
H_DIM = 128
N_MOD = 9

LANES = 128
TOKEN_TILE = 512
FFN_CHUNK = 256
PROJ_CHUNK = 512
M_CHUNK = 128
H_CHUNK = 64
H_BLOCK = 16
H_BLOCK_MAX_EXPONENT = 64.0
HEADS_PER_STEP = 4
VMEM_LIMIT = 56 * 1024 * 1024

_NT = (((1,), (1,)), ((), ()))


def _params(sem, vmem=VMEM_LIMIT):
    return pltpu.CompilerParams(dimension_semantics=sem, vmem_limit_bytes=vmem)


def _resident(shape):
    nd = len(shape)
    return pl.BlockSpec(shape, lambda *_: (0,) * nd, pipeline_mode=pl.Buffered(1))


def _dot(a, b):
    return jnp.dot(a, b, preferred_element_type=F32)


def _rms(x, g):
    ms = jnp.mean(x * x, axis=-1, keepdims=True)
    return x * lax.rsqrt(ms + EPS) * g


def _split3(x):
    hi = x.astype(BF16)
    r = x - hi.astype(F32)
    mid = r.astype(BF16)
    lo = (r - mid.astype(F32)).astype(BF16)
    return hi, mid, lo


def _tri_cumsum(tri, x):
    hi, mid, _ = _split3(x)
    return _dot(tri, hi) + _dot(tri, mid)


def _mod_kernel(c_ref, w_ref, b_ref, o_ref):
    c = c_ref[...]
    a = c * jax.nn.sigmoid(c)
    a_hi = a.astype(BF16)
    a_lo = (a - a_hi.astype(F32)).astype(BF16)
    w = w_ref[...]
    w_hi = w.astype(BF16)
    w_lo = (w - w_hi.astype(F32)).astype(BF16)
    o_ref[...] = _dot(a_hi, w_hi) + _dot(a_hi, w_lo) + _dot(a_lo, w_hi) + b_ref[...]


def _modulation(cc, w, b):
    R, D = cc.shape
    N = w.shape[1]
    tn = 1024
    return pl.pallas_call(
        _mod_kernel,
        out_shape=jax.ShapeDtypeStruct((R, N), F32),
        grid=(N // tn,),
        in_specs=[pl.BlockSpec((R, D), lambda j: (0, 0)),
                  pl.BlockSpec((D, tn), lambda j: (0, j)),
                  pl.BlockSpec((1, tn), lambda j: (0, j))],
        out_specs=pl.BlockSpec((R, tn), lambda j: (0, j)),
        compiler_params=_params(("arbitrary",)),
        name="modulation",
    )(cc, w, b)


def _ffn_kernel(x_ref, mod_ref, g_ref, win_ref, wout_ref, g2_ref, *rest,
                rows, rows2, out_x, aux, n_chunks, merge_gate_row):
    fc = FFN_CHUNK
    x = x_ref[...]
    mod = mod_ref[0]
    outs = rest
    if merge_gate_row is not None:
        (hm_ref, hh_ref, om_ref, oh_ref, gm_ref, gh_ref, nm_ref, nh_ref,
         pm_ref, ph_ref, wo_ref) = rest[:11]
        outs = rest[11:]

        def branch(h_ref, gate_ref, gain_ref):
            heads = []
            for j in range(h_ref.shape[1] // LANES):
                cs = slice(j * LANES, (j + 1) * LANES)
                heads.append(_rms(h_ref[:, cs].astype(F32), gain_ref[:, cs])
                             * gate_ref[:, cs].astype(F32))
            return jnp.concatenate(heads, axis=1).astype(BF16)

        z = (gm_ref[...].astype(F32) * _dot(branch(hm_ref, om_ref, nm_ref), pm_ref[...])
             + gh_ref[...].astype(F32) * _dot(branch(hh_ref, oh_ref, nh_ref), ph_ref[...])
             ).astype(BF16)
        x = x + mod[merge_gate_row:merge_gate_row + 1] * _dot(z, wo_ref[...])
    shift = mod[rows[0]:rows[0] + 1]
    scale = mod[rows[1]:rows[1] + 1]
    gate = mod[rows[2]:rows[2] + 1]
    xm = (_rms(x, g_ref[...]) * (1.0 + scale) + shift).astype(BF16)
    acc = jnp.zeros(x.shape, F32)
    for j in range(n_chunks):
        au = _dot(xm, win_ref[:, j * 2 * fc:(j + 1) * 2 * fc])
        a = au[:, :fc]
        u = au[:, fc:]
        h = (a * jax.nn.sigmoid(a) * u).astype(BF16)
        acc = acc + _dot(h, wout_ref[j * fc:(j + 1) * fc, :])
    xn = x + 0.5 * gate * acc
    k = 0
    if out_x:
        outs[k][...] = xn
        k += 1
    if aux == "mix":
        shift2 = mod[rows2[0]:rows2[0] + 1]
        scale2 = mod[rows2[1]:rows2[1] + 1]
        outs[k][...] = (_rms(xn, g2_ref[...]) * (1.0 + scale2) + shift2).astype(BF16)
    elif aux == "final":
        outs[k][...] = _rms(xn, g2_ref[...])


def _ffn(x2d, mod, tiles_per_mod, g, win, wout, g2, *, rows, rows2, out_x, aux, merge=None):
    N, D = x2d.shape
    F = wout.shape[0]
    tm = TOKEN_TILE
    out_shape, out_specs = [], []
    if out_x:
        out_shape.append(jax.ShapeDtypeStruct((N, D), F32))
        out_specs.append(pl.BlockSpec((tm, D), lambda i: (i, 0)))
    if aux == "mix":
        out_shape.append(jax.ShapeDtypeStruct((N, D), BF16))
        out_specs.append(pl.BlockSpec((tm, D), lambda i: (i, 0)))
    elif aux == "final":
        out_shape.append(jax.ShapeDtypeStruct((N, D), F32))
        out_specs.append(pl.BlockSpec((tm, D), lambda i: (i, 0)))
    if tiles_per_mod is None:
        mod_map = lambda i: (0, 0, 0)
    else:
        mod_map = lambda i: (i // tiles_per_mod, 0, 0)
    kern = functools.partial(_ffn_kernel, rows=rows, rows2=rows2, out_x=out_x, aux=aux,
                             n_chunks=F // FFN_CHUNK,
                             merge_gate_row=None if merge is None else merge[11])
    tile = pl.BlockSpec((tm, D), lambda i: (i, 0))
    in_specs = [tile, pl.BlockSpec((1, N_MOD, D), mod_map), _resident((1, D)),
                _resident(win.shape), _resident(wout.shape), _resident((1, D))]
    args = [x2d, mod, g, win, wout, g2]
    if merge is not None:
        in_specs += [tile] * 6 + [_resident(w.shape) for w in merge[6:11]]
        args += list(merge[:11])
    return pl.pallas_call(
        kern,
        out_shape=out_shape,
        grid=(N // tm,),
        in_specs=in_specs,
        out_specs=out_specs,
        compiler_params=_params(("parallel",)),
        name=("merge_" if merge is not None else "") + "ffn_" + aux,
    )(*args)


def _proj_kernel(x_ref, w_ref, b_ref, lb_ref, *outs, segs):
    x = x_ref[...]
    c0 = 0
    oi = 0
    for wd, kind, arg in segs:
        for s in range(0, wd, PROJ_CHUNK):
            w = min(PROJ_CHUNK, wd - s)
            y = _dot(x, w_ref[:, c0 + s:c0 + s + w]) + b_ref[:, c0 + s:c0 + s + w]
            if kind == "bf16":
                outs[oi][:, s:s + w] = y.astype(BF16)
            elif kind == "bf16_t":
                outs[oi][s:s + w, :] = y.T.astype(BF16)
            elif kind == "f32":
                outs[oi][:, s:s + w] = y
            elif kind == "sigmoid":
                outs[oi][:, s:s + w] = jax.nn.sigmoid(y).astype(BF16)
            elif kind == "silu":
                outs[oi][:, s:s + w] = (y * jax.nn.sigmoid(y)).astype(BF16)
            elif kind == "hgrn":
                lb = lb_ref[arg:arg + 1, s:s + w]
                sg = jax.nn.sigmoid(y)
                outs[oi][:, s:s + w] = jnp.log(lb + (1.0 - lb) * sg)
                outs[oi + 1][:, s:s + w] = ((1.0 - lb) * (1.0 - sg)).astype(BF16)
        c0 += wd
        oi += 2 if kind == "hgrn" else 1


def _proj(x2d, w, b, lb, segs, name):
    N, D = x2d.shape
    tm = TOKEN_TILE
    out_shape, out_specs = [], []
    for wd, kind, _ in segs:
        if kind == "bf16_t":
            out_shape.append(jax.ShapeDtypeStruct((wd, N), BF16))
            out_specs.append(pl.BlockSpec((wd, tm), lambda i: (0, i)))
            continue
        dts = {"bf16": [BF16], "f32": [F32], "sigmoid": [BF16], "silu": [BF16],
               "hgrn": [F32, BF16]}[kind]
        for dt in dts:
            out_shape.append(jax.ShapeDtypeStruct((N, wd), dt))
            out_specs.append(pl.BlockSpec((tm, wd), lambda i: (i, 0)))
    return pl.pallas_call(
        functools.partial(_proj_kernel, segs=tuple(segs)),
        out_shape=out_shape,
        grid=(N // tm,),
        in_specs=[pl.BlockSpec((tm, D), lambda i: (i, 0)),
                  _resident(w.shape),
                  _resident(b.shape),
                  _resident(lb.shape)],
        out_specs=out_specs,
        compiler_params=_params(("parallel",)),
        name=name,
    )(x2d, w, b, lb)


def _mlstm_kernel(q_ref, kt_ref, v_ref, g_ref, ktc_ref, vc_ref, gc_ref,
                  out_ref, hacc_ref, c_ref, m_ref, rows_ref,
                  s0_ref, wi0_ref, fl0_ref, s1_ref, wi1_ref, fl1_ref):
    L = M_CHUNK
    hps = HEADS_PER_STEP
    T = q_ref.shape[0]
    Tc = vc_ref.shape[0]
    nt, nct = T // L, Tc // L
    assert nt % 2 == 0
    row = lax.broadcasted_iota(jnp.int32, (L, L), 0)
    col = lax.broadcasted_iota(jnp.int32, (L, L), 1)
    lane = lax.broadcasted_iota(jnp.int32, (L, LANES), 1)
    krow = lax.broadcasted_iota(jnp.int32, (LANES, 1), 0)
    ones_ext = jnp.ones((L, LANES), BF16)
    R_R, R_LF, R_W, R_M, R_DEC = (k * hps for k in range(5))

    for d in (0, 1):
        tri = (col <= row) if d == 0 else (col >= row)
        trif_t = ((col >= row) if d == 0 else (col <= row)).astype(F32)
        c_ref[...] = jnp.zeros(c_ref.shape, F32)
        m_ref[...] = jnp.zeros(m_ref.shape, F32)

        def gate_body(n, carry, gref, slot0, d=d, trif_t=trif_t):
            GT = gref[pl.ds(pl.multiple_of(n * L, L), L), :].T[:4 * hps, :]
            lf = jax.nn.log_sigmoid(GT[(2 + d) * hps:(3 + d) * hps, :])
            b = sum(_dot(part.astype(F32), trif_t) for part in _split3(lf))
            r = GT[d * hps:(d + 1) * hps, :] - b
            b_last = b[:, L - 1:L] if d == 0 else b[:, 0:1]
            wlog = b_last + r
            wide = lambda x: jnp.broadcast_to(x, (hps, L))
            rows_ref[slot0 + n] = jnp.concatenate(
                [r, lf, wlog, wide(b_last), wide(jnp.max(wlog, axis=1, keepdims=True)),
                 jnp.zeros((rows_ref.shape[1] - 5 * hps, L), F32)], axis=0)
            return carry

        def stab_body(n, carry, nchunks, slot0, d=d):
            idx = slot0 + (n if d == 0 else nchunks - 1 - n)
            R = rows_ref[idx]
            m_prev = m_ref[0:hps, 0:1]
            bm = R[R_M:R_M + hps, 0:1] + m_prev
            m_new = jnp.maximum(bm, R[R_DEC:R_DEC + hps, 0:1])
            wide = lambda x: jnp.broadcast_to(x, (hps, L))
            rows_ref[idx, R_W:R_W + 3 * hps, :] = jnp.concatenate(
                [jnp.exp(R[R_W:R_W + hps, :] - m_new), wide(m_prev), wide(jnp.exp(bm - m_new))], axis=0)
            m_ref[0:hps, :] = jnp.broadcast_to(m_new, (hps, LANES))
            return carry

        lax.fori_loop(0, nct, functools.partial(gate_body, gref=gc_ref, slot0=0), 0, unroll=True)
        lax.fori_loop(0, nt, functools.partial(gate_body, gref=g_ref, slot0=nct), 0, unroll=8)
        lax.fori_loop(0, nct, functools.partial(stab_body, nchunks=nct, slot0=0), 0)
        lax.fori_loop(0, nt, functools.partial(stab_body, nchunks=nt, slot0=nct), 0, unroll=4)

        def update_state(p, R, rows, ktref, vref):
            kT = ktref[p * LANES:(p + 1) * LANES, rows].astype(F32)
            upd = None
            decay_rows = None
            for a in range(2):
                i = 2 * p + a
                vext = jnp.concatenate([vref[rows, i * LANES:(i + 1) * LANES], ones_ext], axis=1)
                rowmask = (krow < M_QK_DIM) if a == 0 else (krow >= M_QK_DIM)
                kw = jnp.where(rowmask, kT * R[R_W + i:R_W + i + 1, :], 0.0)
                u = _dot(kw.astype(BF16).astype(F32), vext.astype(F32))
                upd = u if upd is None else upd + u
                dr = jnp.where(rowmask, R[R_DEC + i:R_DEC + i + 1, 0:1], 0.0)
                decay_rows = dr if decay_rows is None else decay_rows + dr
            c_ref[p] = decay_rows * c_ref[p] + upd

        def ctx_body(n, carry, d=d, update_state=update_state):
            c = n if d == 0 else nct - 1 - n
            rows = pl.ds(pl.multiple_of(c * L, L), L)
            for p in range(hps // 2):
                update_state(p, rows_ref[c], rows, ktc_ref, vc_ref)
            return carry

        lax.fori_loop(0, nct, ctx_body, 0)

        def head_q(rows, p, a):
            q2 = q_ref[rows, p * LANES:(p + 1) * LANES]
            headmask = (lane < M_QK_DIM) if a == 0 else (lane >= M_QK_DIM)
            return jnp.where(headmask, q2, jnp.zeros_like(q2))

        def pair_weights(c, s_ref, wi_ref, fl_ref, tri=tri, head_q=head_q):
            rows = pl.ds(pl.multiple_of(c * L, L), L)
            R = rows_ref[nct + c]
            for i in range(hps):
                p, a = divmod(i, 2)
                s_raw = _dot(head_q(rows, p, a), kt_ref[p * LANES:(p + 1) * LANES, rows])
                m_prev = R[R_M + i:R_M + i + 1, 0:1]
                rm = jnp.where(tri, R[R_R + i:R_R + i + 1, :], -jnp.inf)
                mu = jnp.maximum(m_prev, jnp.max(rm, axis=1, keepdims=True))
                bcol = jnp.sum(jnp.where(tri, R[R_LF + i:R_LF + i + 1, :], 0.0), axis=1, keepdims=True)
                mu_b = jnp.broadcast_to(mu, (L, L))
                s_ref[i] = s_raw * jnp.exp(rm - mu_b)
                wi_ref[i] = jnp.exp(m_prev - mu_b)
                fl_ref[i] = jnp.exp(-jnp.broadcast_to(bcol + mu, (L, LANES)))

        def emit(c, s_ref, wi_ref, fl_ref, d=d, head_q=head_q, update_state=update_state):
            rows = pl.ds(pl.multiple_of(c * L, L), L)
            for p in range(hps // 2):
                C2b = c_ref[p].astype(BF16)
                for a in range(2):
                    i = 2 * p + a
                    cs = slice(i * LANES, (i + 1) * LANES)
                    vext = jnp.concatenate([v_ref[rows, cs], ones_ext], axis=1)
                    nd = (jnp.concatenate([wi_ref[i]] * 2, axis=1) * _dot(head_q(rows, p, a), C2b)
                          + _dot(s_ref[i].astype(BF16), vext))
                    h = nd[:, :LANES] / jnp.maximum(jnp.abs(nd[:, LANES:]), fl_ref[i])
                    if d == 0:
                        hacc_ref[rows, cs] = h
                    else:
                        out_ref[rows, cs] = (hacc_ref[rows, cs] + h).astype(BF16)
                update_state(p, rows_ref[nct + c], rows, kt_ref, v_ref)

        order = (lambda n: n) if d == 0 else (lambda n: nt - 1 - n)
        slot0 = (s0_ref, wi0_ref, fl0_ref)
        slot1 = (s1_ref, wi1_ref, fl1_ref)

        def lat_body(n, carry, pair_weights=pair_weights, emit=emit, order=order):
            c0, c1 = 2 * n, 2 * n + 1
            c2 = jnp.minimum(2 * n + 2, nt - 1)
            pair_weights(order(c1), *slot1)
            emit(order(c0), *slot0)
            pair_weights(order(c2), *slot0)
            emit(order(c1), *slot1)
            return carry

        pair_weights(order(0), *slot0)
        lax.fori_loop(0, nt // 2, lat_body, 0)


def _mlstm(q, kt, v, g, ktc, vc, gc):
    B, T, _ = q.shape
    Tc = vc.shape[1]
    hps = HEADS_PER_STEP
    G = M_HEADS // hps
    wq, wv = hps * M_QK_DIM, hps * M_V_DIM
    lat = lambda w: pl.BlockSpec((None, T, w), lambda b, g: (b, 0, g))
    cx = lambda w: pl.BlockSpec((None, Tc, w), lambda b, g: (b, 0, g))
    return pl.pallas_call(
        _mlstm_kernel,
        out_shape=jax.ShapeDtypeStruct((B, T, M_HEADS * M_V_DIM), BF16),
        grid=(B, G),
        in_specs=[lat(wq), pl.BlockSpec((wq, T), lambda b, g: (g, b)), lat(wv), lat(LANES),
                  pl.BlockSpec((wq, Tc), lambda b, g: (g, b)), cx(wv), cx(LANES)],
        out_specs=lat(wv),
        scratch_shapes=[pltpu.VMEM((T, wv), F32),
                        pltpu.VMEM((hps // 2, 2 * M_QK_DIM, 2 * LANES), F32),
                        pltpu.VMEM((8, LANES), F32),
                        pltpu.VMEM(((T + Tc) // M_CHUNK, 8 * hps, M_CHUNK), F32)]
                       + [pltpu.VMEM((hps, M_CHUNK, M_CHUNK), F32)] * 6,
        compiler_params=_params(("parallel", "parallel")),
        name="mlstm_scan",
    )(q, kt, v, g, ktc, vc, gc)


def _pivot_rows(b, h, off):
    L, N = b.shape
    bc = lambda r, n: jnp.broadcast_to(b[r:r + 1, :], (n, N))
    if 2 * h >= 8:
        return jnp.concatenate([bc(x + off, 2 * h) for x in range(0, L, 2 * h)], axis=0)
    sub = lax.broadcasted_iota(jnp.int32, (8, N), 0)
    tiles = []
    for x in range(0, L, 8):
        t = bc(x + 8 - 2 * h + off, 8)
        for y in range(8 - 4 * h, -1, -2 * h):
            t = jnp.where(sub < y + 2 * h, bc(x + y + off, 8), t)
        tiles.append(t)
    return jnp.concatenate(tiles, axis=0)


def _block_entry_rows(b, beta, d):
    L, N = b.shape
    tiles = []
    for x in range(0, L, beta):
        r = x - 1 if d == 0 else x + beta
        inside = 0 <= r < L
        tiles.append(jnp.broadcast_to(b[r:r + 1, :], (beta, N)) if inside else jnp.zeros((beta, N), F32))
    return jnp.concatenate(tiles, axis=0)


def _hgrn_kernel(q_ref, kf_ref, kb_ref, v_ref, lff_ref, lfb_ref,
                 kfc_ref, kbc_ref, vc_ref, lffc_ref, lfbc_ref,
                 out_ref, hacc_ref, st_ref, b_ref, bc_ref, attn0_ref, attn1_ref, *, beta):
    L = H_CHUNK
    hps = HEADS_PER_STEP
    T = q_ref.shape[0]
    Tc = vc_ref.shape[0]
    nt, nct = T // L, Tc // L
    assert nt % 2 == 0
    row = lax.broadcasted_iota(jnp.int32, (L, L), 0)
    col = lax.broadcasted_iota(jnp.int32, (L, L), 1)
    halves = [h for h in (L >> (n + 1) for n in range(L.bit_length() - 1)) if h >= beta]
    zpad_f = jnp.zeros((H_DIM - L, H_DIM), F32)
    zpad_b = jnp.zeros((H_DIM - L, H_DIM), BF16)
    neg_log2e = -1.4426950408889634

    for d in (0, 1):
        tri = (col <= row) if d == 0 else (col >= row)
        trib = tri.astype(BF16)
        last = L - 1 if d == 0 else 0
        st_ref[...] = jnp.zeros(st_ref.shape, F32)
        k_lat, lf_lat = (kf_ref, lff_ref) if d == 0 else (kb_ref, lfb_ref)
        k_ctx, lf_ctx = (kfc_ref, lffc_ref) if d == 0 else (kbc_ref, lfbc_ref)
        local = tri & ((row & -beta) == (col & -beta))
        pair = []
        for h in halves:
            same = (row & -(2 * h)) == (col & -(2 * h))
            r_late = (row & (2 * h - 1)) >= h
            c_late = (col & (2 * h - 1)) >= h
            pair.append(same & (r_late & ~c_late if d == 0 else ~r_late & c_late))

        def cumsum_body(n, carry, lfref, bref, trib=trib):
            rows = pl.ds(pl.multiple_of(n * L, L), L)
            bref[rows, :] = _tri_cumsum(trib, lfref[rows, :])
            return carry

        lax.fori_loop(0, nct, functools.partial(cumsum_body, lfref=lf_ctx, bref=bc_ref), 0, unroll=True)
        lax.fori_loop(0, nt, functools.partial(cumsum_body, lfref=lf_lat, bref=b_ref), 0, unroll=8)

        def update_state(i, rows, cs, kref, vref, bref):
            b_h = bref[rows, cs]
            bl = b_h[last:last + 1, :]
            kd = (kref[rows, cs].astype(F32) * jnp.exp(bl - b_h)).astype(BF16)
            vT = jnp.concatenate([vref[rows, cs].astype(F32), zpad_f], axis=0).T.astype(BF16)
            st_ref[i] = (st_ref[i] * jnp.exp(bl)
                         + _dot(vT, jnp.concatenate([kd, zpad_b], axis=0)))

        def ctx_body(n, carry, d=d, k_ctx=k_ctx, update_state=update_state):
            c = n if d == 0 else nct - 1 - n
            rows = pl.ds(pl.multiple_of(c * L, L), L)
            for i in range(hps):
                update_state(i, rows, slice(i * H_DIM, (i + 1) * H_DIM), k_ctx, vc_ref, bc_ref)
            return carry

        lax.fori_loop(0, nct, ctx_body, 0)

        def pair_weights(c, attn_ref, d=d, k_lat=k_lat, local=local, pair=pair):
            rows = pl.ds(pl.multiple_of(c * L, L), L)
            for i in range(hps):
                cs = slice(i * H_DIM, (i + 1) * H_DIM)
                b_h = b_ref[rows, cs]
                q_h = q_ref[rows, cs].astype(F32)
                k_h = k_lat[rows, cs].astype(F32)
                if beta > 1:
                    dl = (b_h - _block_entry_rows(b_h, beta, d)) * neg_log2e
                    qs = (q_h * jnp.exp2(-dl)).astype(BF16)
                    ks = (k_h * jnp.exp2(dl)).astype(BF16)
                else:
                    qs, ks = q_ref[rows, cs], k_lat[rows, cs]
                attn = jnp.where(local, lax.dot_general(qs, ks, _NT, preferred_element_type=F32), 0.0)
                for n, h in enumerate(halves):
                    z = jnp.exp2(jnp.abs(b_h - _pivot_rows(b_h, h, h - 1 + d)) * neg_log2e)
                    qk = lax.dot_general((q_h * z).astype(BF16), (k_h * z).astype(BF16), _NT,
                                         preferred_element_type=F32)
                    attn = jnp.where(pair[n], qk, attn)
                attn_ref[i] = attn

        def emit(c, attn_ref, d=d, k_lat=k_lat, update_state=update_state):
            rows = pl.ds(pl.multiple_of(c * L, L), L)
            for i in range(hps):
                cs = slice(i * H_DIM, (i + 1) * H_DIM)
                qc = (q_ref[rows, cs].astype(F32) * jnp.exp(b_ref[rows, cs])).astype(BF16)
                o = (lax.dot_general(qc, st_ref[i].astype(BF16), _NT, preferred_element_type=F32)
                     + _dot(attn_ref[i].astype(BF16), v_ref[rows, cs]))
                if d == 0:
                    hacc_ref[rows, cs] = o
                else:
                    out_ref[rows, cs] = (hacc_ref[rows, cs] + o).astype(BF16)
                update_state(i, rows, cs, k_lat, v_ref, b_ref)

        order = (lambda n: n) if d == 0 else (lambda n: nt - 1 - n)

        def lat_body(n, carry, pair_weights=pair_weights, emit=emit, order=order):
            c0, c1 = 2 * n, 2 * n + 1
            c2 = jnp.minimum(2 * n + 2, nt - 1)
            pair_weights(order(c1), attn1_ref)
            emit(order(c0), attn0_ref)
            pair_weights(order(c2), attn0_ref)
            emit(order(c1), attn1_ref)
            return carry

        pair_weights(order(0), attn0_ref)
        lax.fori_loop(0, nt // 2, lat_body, 0)


def _hgrn(beta, q, kf, kb, v, lff, lfb, kfc, kbc, vc, lffc, lfbc):
    B, T, W = q.shape
    Tc = vc.shape[1]
    hps = HEADS_PER_STEP
    G = H_HEADS // hps
    w = hps * H_DIM
    lat = pl.BlockSpec((None, T, w), lambda b, g: (b, 0, g))
    cx = pl.BlockSpec((None, Tc, w), lambda b, g: (b, 0, g))
    return pl.pallas_call(
        functools.partial(_hgrn_kernel, beta=beta),
        out_shape=jax.ShapeDtypeStruct((B, T, W), BF16),
        grid=(B, G),
        in_specs=[lat] * 6 + [cx] * 5,
        out_specs=lat,
        scratch_shapes=[pltpu.VMEM((T, w), F32),
                        pltpu.VMEM((hps, H_DIM, H_DIM), F32),
                        pltpu.VMEM((T, w), F32),
                        pltpu.VMEM((Tc, w), F32),
                        pltpu.VMEM((hps, H_CHUNK, H_CHUNK), F32),
                        pltpu.VMEM((hps, H_CHUNK, H_CHUNK), F32)],
        compiler_params=_params(("parallel", "parallel")),
        name="hgrn_scan_b%d" % beta,
    )(q, kf, kb, v, lff, lfb, kfc, kbc, vc, lffc, lfbc)


def _interleave_ffn_in(w):
    D, F2 = w.shape
    F = F2 // 2
    a = w[:, :F].reshape(D, F // FFN_CHUNK, 1, FFN_CHUNK)
    u = w[:, F:].reshape(D, F // FFN_CHUNK, 1, FFN_CHUNK)
    return jnp.concatenate([a, u], axis=2).reshape(D, F2)


def _to_col_major(a):
    B, T, C = a.shape
    return a.reshape(B, T // GRID_W, GRID_W, C).transpose(0, 2, 1, 3).reshape(B, T, C)


def _to_row_major(a):
    B, T, C = a.shape
    return a.reshape(B, GRID_W, T // GRID_W, C).transpose(0, 2, 1, 3).reshape(B, T, C)


def kernel(x, c, ctx, c_ctx, ada_w, ada_b, ffn1_norm, ffn1_w_in, ffn1_w_out, mix_norm,
           mix_w_in, mix_b_in, mlstm_norm, hgrn_lb_logits, hgrn_norm, proj_m, proj_h,
           mix_w_out, ffn2_norm, ffn2_w_in, ffn2_w_out, final_norm):
    B, T, D = x.shape
    Tc = ctx.shape[1]
    hps = HEADS_PER_STEP
    MQ, MV, HW = M_HEADS * M_QK_DIM, M_HEADS * M_V_DIM, H_HEADS * H_DIM

    pad = (-(B + 1)) % 8
    cc = jnp.concatenate([c, c_ctx[None, :], jnp.zeros((pad, D), F32)], axis=0)
    mod = _modulation(cc, ada_w[0], ada_b[0][None, :])
    ml = mod[:B].reshape(B, N_MOD, D)
    mc = mod[B:B + 1].reshape(1, N_MOD, D)

    row = lambda v: v.reshape(1, -1).astype(F32)
    w1_in = _interleave_ffn_in(ffn1_w_in[0]).astype(BF16)
    w1_out = ffn1_w_out[0].astype(BF16)
    w2_in = _interleave_ffn_in(ffn2_w_in[0]).astype(BF16)
    w2_out = ffn2_w_out[0].astype(BF16)

    W, bias = mix_w_in[0], mix_b_in[0]
    o = 0
    seg = {}
    for name, wd in (("mq", MQ), ("mk", MQ), ("mv", MV), ("mo", MV), ("ig", 2 * M_HEADS),
                     ("fg", 2 * M_HEADS), ("hq", HW), ("hff", HW), ("hfb", HW), ("hi", HW),
                     ("hg", HW), ("gm", D), ("gh", D)):
        seg[name] = (o, o + wd)
        o += wd
    gate_cols = []
    for g in range(M_HEADS // hps):
        gate_cols.append([(seg[n][0] + o + g * hps, seg[n][0] + o + (g + 1) * hps)
                          for n in ("ig", "fg") for o in (0, M_HEADS)])

    def gather(parts, scale_q=False):
        ws, bs = [], []
        for p in parts:
            if p == "gates":
                for ranges in gate_cols:
                    ws += [W[:, a:b] for a, b in ranges] + [jnp.zeros((D, LANES - 4 * hps), F32)]
                    bs += [bias[a:b] for a, b in ranges] + [jnp.zeros((LANES - 4 * hps,), F32)]
            else:
                s = M_QK_DIM ** -0.5 if p == "mq" else 1.0
                ws.append(W[:, seg[p][0]:seg[p][1]] * s)
                bs.append(bias[seg[p][0]:seg[p][1]] * s)
        return jnp.concatenate(ws, axis=1).astype(BF16), jnp.concatenate(bs)[None, :]

    n_gate = LANES * (M_HEADS // hps)
    lb = jnp.cumsum(jax.nn.softmax(hgrn_lb_logits.astype(F32), axis=1), axis=1)[:, 0]

    x2d = x.reshape(B * T, D)
    c2d = ctx.reshape(B * Tc, D)
    tpb = T // TOKEN_TILE
    (xm_c,) = _ffn(c2d, mc, None, row(ffn1_norm[0]), w1_in, w1_out, row(mix_norm[0]),
                   rows=(0, 1, 2), rows2=(3, 4), out_x=False, aux="mix")
    x1, xm_l = _ffn(x2d, ml, tpb, row(ffn1_norm[0]), w1_in, w1_out, row(mix_norm[0]),
                    rows=(0, 1, 2), rows2=(3, 4), out_x=True, aux="mix")

    wm_l, bm_l = gather(["mq", "mk", "mv", "mo", "gates", "gm", "gh", "hg"])
    mq, mk, mv, mso, mg, sgm, sgh, hsg = _proj(
        xm_l, wm_l, bm_l, lb,
        [(MQ, "bf16", 0), (MQ, "bf16_t", 0), (MV, "bf16", 0), (MV, "sigmoid", 0),
         (n_gate, "f32", 0), (D, "sigmoid", 0), (D, "sigmoid", 0), (HW, "silu", 0)], "proj_mlstm")
    wm_c, bm_c = gather(["mk", "mv", "gates"])
    mkc, mvc, mgc = _proj(xm_c, wm_c, bm_c, lb,
                          [(MQ, "bf16_t", 0), (MV, "bf16", 0), (n_gate, "f32", 0)], "proj_mlstm_ctx")

    xm_cm = _to_col_major(xm_l.reshape(B, T, D)).reshape(B * T, D)
    wh_l, bh_l = gather(["hq", "hff", "hfb", "hi"])
    hq, lff, hkf, lfb, hkb, hv = _proj(
        xm_cm, wh_l, bh_l, lb,
        [(HW, "silu", 0), (HW, "hgrn", 0), (HW, "hgrn", 1), (HW, "bf16", 0)], "proj_hgrn")
    wh_c, bh_c = gather(["hff", "hfb", "hi"])
    lffc, hkfc, lfbc, hkbc, hvc = _proj(
        xm_c, wh_c, bh_c, lb, [(HW, "hgrn", 0), (HW, "hgrn", 1), (HW, "bf16", 0)], "proj_hgrn_ctx")

    r3 = lambda a, t: a.reshape(B, t, a.shape[-1])
    hm = _mlstm(r3(mq, T), mk, r3(mv, T), r3(mg, T), mkc, r3(mvc, Tc), r3(mgc, Tc))
    hgrn_args = (r3(hq, T), r3(hkf, T), r3(hkb, T), r3(hv, T), r3(lff, T), r3(lfb, T),
                 r3(hkfc, Tc), r3(hkbc, Tc), r3(hvc, Tc), r3(lffc, Tc), r3(lfbc, Tc))
    block_ok = H_BLOCK * jnp.max(-jnp.log(lb)) <= H_BLOCK_MAX_EXPONENT
    hh_cm = lax.cond(block_ok, functools.partial(_hgrn, H_BLOCK), functools.partial(_hgrn, 1),
                     *hgrn_args)
    hh = _to_row_major(hh_cm)

    merge = (hm.reshape(B * T, MV), hh.reshape(B * T, HW), mso, hsg, sgm, sgh,
             row(mlstm_norm[0]), row(hgrn_norm[0]),
             proj_m[0].astype(BF16), proj_h[0].astype(BF16), mix_w_out[0].astype(BF16), 5)
    (out,) = _ffn(x1, ml, tpb, row(ffn2_norm[0]), w2_in, w2_out, row(final_norm),
                  rows=(6, 7, 8), rows2=(3, 4), out_x=False, aux="final", merge=merge)
    return out.reshape(B, T, D)
```

```python
import functools

import jax
import jax.numpy as jnp
from jax import lax
from jax.experimental import pallas as pl
from jax.experimental.pallas import tpu as pltpu

F32 = jnp.float32
BF16 = jnp.bfloat16
EPS = 1e-6

GRID_W = 64
M_HEADS = 8
M_QK_DIM = 64
M_V_DIM = 128
H_HEADS = 8
H_DIM = 128
N_MOD = 9

LANES = 128
TOKEN_TILE = 512
FFN_CHUNK = 256
PROJ_CHUNK = 512
M_CHUNK = 128
H_CHUNK = 64
H_BLOCK = 16
H_BLOCK_MAX_EXPONENT = 64.0
HEADS_PER_STEP = 4
SCAN_BODY_CHUNKS = 4
VMEM_LIMIT = 56 * 1024 * 1024

_NT = (((1,), (1,)), ((), ()))


def _params(sem, vmem=VMEM_LIMIT):
    return pltpu.CompilerParams(dimension_semantics=sem, vmem_limit_bytes=vmem)


def _resident(shape):
    nd = len(shape)
    return pl.BlockSpec(shape, lambda *_: (0,) * nd, pipeline_mode=pl.Buffered(1))


def _dot(a, b):
    return jnp.dot(a, b, preferred_element_type=F32)


def _rms(x, g):
    ms = jnp.mean(x * x, axis=-1, keepdims=True)
    return x * lax.rsqrt(ms + EPS) * g


def _split3(x):
    hi = x.astype(BF16)
    r = x - hi.astype(F32)
    mid = r.astype(BF16)
    lo = (r - mid.astype(F32)).astype(BF16)
    return hi, mid, lo


def _tri_cumsum(tri, x):
    hi, mid, _ = _split3(x)
    return _dot(tri, hi) + _dot(tri, mid)


def _mod_kernel(c_ref, w_ref, b_ref, o_ref):
    c = c_ref[...]
    a = c * jax.nn.sigmoid(c)
    a_hi = a.astype(BF16)
    a_lo = (a - a_hi.astype(F32)).astype(BF16)
    w = w_ref[...]
    w_hi = w.astype(BF16)
    w_lo = (w - w_hi.astype(F32)).astype(BF16)
    o_ref[...] = _dot(a_hi, w_hi) + _dot(a_hi, w_lo) + _dot(a_lo, w_hi) + b_ref[...]


def _modulation(cc, w, b):
    R, D = cc.shape
    N = w.shape[1]
    tn = 1024
    return pl.pallas_call(
        _mod_kernel,
        out_shape=jax.ShapeDtypeStruct((R, N), F32),
        grid=(N // tn,),
        in_specs=[pl.BlockSpec((R, D), lambda j: (0, 0)),
                  pl.BlockSpec((D, tn), lambda j: (0, j)),
                  pl.BlockSpec((1, tn), lambda j: (0, j))],
        out_specs=pl.BlockSpec((R, tn), lambda j: (0, j)),
        compiler_params=_params(("arbitrary",)),
        name="modulation",
    )(cc, w, b)


def _ffn_kernel(x_ref, mod_ref, g_ref, win_ref, wout_ref, g2_ref, *rest,
                rows, rows2, out_x, aux, n_chunks, merge_gate_row):
    fc = FFN_CHUNK
    x = x_ref[...]
    mod = mod_ref[0]
    outs = rest
    if merge_gate_row is not None:
        (hm_ref, hh_ref, om_ref, oh_ref, gm_ref, gh_ref, nm_ref, nh_ref,
         pm_ref, ph_ref, wo_ref) = rest[:11]
        outs = rest[11:]

        def branch(h_ref, gate_ref, gain_ref):
            heads = []
            for j in range(h_ref.shape[1] // LANES):
                cs = slice(j * LANES, (j + 1) * LANES)
                heads.append(_rms(h_ref[:, cs].astype(F32), gain_ref[:, cs])
                             * gate_ref[:, cs].astype(F32))
            return jnp.concatenate(heads, axis=1).astype(BF16)

        z = (gm_ref[...].astype(F32) * _dot(branch(hm_ref, om_ref, nm_ref), pm_ref[...])
             + gh_ref[...].astype(F32) * _dot(branch(hh_ref, oh_ref, nh_ref), ph_ref[...])
             ).astype(BF16)
        x = x + mod[merge_gate_row:merge_gate_row + 1] * _dot(z, wo_ref[...])
    shift = mod[rows[0]:rows[0] + 1]
    scale = mod[rows[1]:rows[1] + 1]
    gate = mod[rows[2]:rows[2] + 1]
    xm = (_rms(x, g_ref[...]) * (1.0 + scale) + shift).astype(BF16)
    acc = jnp.zeros(x.shape, F32)
    F = n_chunks * fc
    for j in range(n_chunks):
        a = _dot(xm, win_ref[:, j * fc:(j + 1) * fc])
        u = _dot(xm, win_ref[:, F + j * fc:F + (j + 1) * fc])
        h = (a * jax.nn.sigmoid(a) * u).astype(BF16)
        acc = acc + _dot(h, wout_ref[j * fc:(j + 1) * fc, :])
    xn = x + 0.5 * gate * acc
    k = 0
    if out_x:
        outs[k][...] = xn
        k += 1
    if aux == "mix":
        shift2 = mod[rows2[0]:rows2[0] + 1]
        scale2 = mod[rows2[1]:rows2[1] + 1]
        outs[k][...] = (_rms(xn, g2_ref[...]) * (1.0 + scale2) + shift2).astype(BF16)
    elif aux == "final":
        outs[k][...] = _rms(xn, g2_ref[...])


def _ffn(x2d, mod, tiles_per_mod, g, win, wout, g2, *, rows, rows2, out_x, aux, merge=None):
    N, D = x2d.shape
    F = wout.shape[0]
    tm = TOKEN_TILE
    out_shape, out_specs = [], []
    if out_x:
        out_shape.append(jax.ShapeDtypeStruct((N, D), F32))
        out_specs.append(pl.BlockSpec((tm, D), lambda i: (i, 0)))
    if aux == "mix":
        out_shape.append(jax.ShapeDtypeStruct((N, D), BF16))
        out_specs.append(pl.BlockSpec((tm, D), lambda i: (i, 0)))
    elif aux == "final":
        out_shape.append(jax.ShapeDtypeStruct((N, D), F32))
        out_specs.append(pl.BlockSpec((tm, D), lambda i: (i, 0)))
    if tiles_per_mod is None:
        mod_map = lambda i: (0, 0, 0)
    else:
        mod_map = lambda i: (i // tiles_per_mod, 0, 0)
    kern = functools.partial(_ffn_kernel, rows=rows, rows2=rows2, out_x=out_x, aux=aux,
                             n_chunks=F // FFN_CHUNK,
                             merge_gate_row=None if merge is None else merge[11])
    tile = pl.BlockSpec((tm, D), lambda i: (i, 0))
    in_specs = [tile, pl.BlockSpec((1, N_MOD, D), mod_map), _resident((1, D)),
                _resident(win.shape), _resident(wout.shape), _resident((1, D))]
    args = [x2d, mod, g, win, wout, g2]
    if merge is not None:
        in_specs += [tile] * 6 + [_resident(w.shape) for w in merge[6:11]]
        args += list(merge[:11])
    return pl.pallas_call(
        kern,
        out_shape=out_shape,
        grid=(N // tm,),
        in_specs=in_specs,
        out_specs=out_specs,
        compiler_params=_params(("parallel",)),
        name=("merge_" if merge is not None else "") + "ffn_" + aux,
    )(*args)


def _proj_kernel(x_ref, w_ref, b_ref, lb_ref, *outs, segs):
    x = x_ref[...]
    oi = 0
    for c0, wd, kind, arg in segs:
        for s in range(0, wd, PROJ_CHUNK):
            w = min(PROJ_CHUNK, wd - s)
            y = _dot(x, w_ref[:, c0 + s:c0 + s + w]) + b_ref[:, c0 + s:c0 + s + w]
            if kind == "bf16":
                outs[oi][:, s:s + w] = y.astype(BF16)
            elif kind == "bf16_t":
                outs[oi][s:s + w, :] = y.T.astype(BF16)
            elif kind == "f32":
                outs[oi][:, s:s + w] = y
            elif kind == "sigmoid":
                outs[oi][:, s:s + w] = jax.nn.sigmoid(y).astype(BF16)
            elif kind == "silu":
                outs[oi][:, s:s + w] = (y * jax.nn.sigmoid(y)).astype(BF16)
            elif kind == "hgrn":
                lb = lb_ref[arg:arg + 1, s:s + w]
                sg = jax.nn.sigmoid(y)
                outs[oi][:, s:s + w] = jnp.log(lb + (1.0 - lb) * sg)
                outs[oi + 1][:, s:s + w] = ((1.0 - lb) * (1.0 - sg)).astype(BF16)
        oi += 2 if kind == "hgrn" else 1


def _proj(x2d, w, b, lb, segs, name):
    N, D = x2d.shape
    tm = TOKEN_TILE
    out_shape, out_specs = [], []
    for _, wd, kind, _ in segs:
        if kind == "bf16_t":
            out_shape.append(jax.ShapeDtypeStruct((wd, N), BF16))
            out_specs.append(pl.BlockSpec((wd, tm), lambda i: (0, i)))
            continue
        dts = {"bf16": [BF16], "f32": [F32], "sigmoid": [BF16], "silu": [BF16],
               "hgrn": [F32, BF16]}[kind]
        for dt in dts:
            out_shape.append(jax.ShapeDtypeStruct((N, wd), dt))
            out_specs.append(pl.BlockSpec((tm, wd), lambda i: (i, 0)))
    return pl.pallas_call(
        functools.partial(_proj_kernel, segs=tuple(segs)),
        out_shape=out_shape,
        grid=(N // tm,),
        in_specs=[pl.BlockSpec((tm, D), lambda i: (i, 0)),
                  _resident(w.shape),
                  _resident(b.shape),
                  _resident(lb.shape)],
        out_specs=out_specs,
        compiler_params=_params(("parallel",)),
        name=name,
    )(x2d, w, b, lb)


def _mlstm_kernel(q_ref, kt_ref, v_ref, g_ref, ktc_ref, vc_ref, gc_ref,
                  out_ref, hacc_ref, c_ref, m_ref, rows_ref,
                  s0_ref, wi0_ref, fl0_ref, s1_ref, wi1_ref, fl1_ref):
    L = M_CHUNK
    hps = HEADS_PER_STEP
    T = q_ref.shape[0]
    Tc = vc_ref.shape[0]
    nt, nct = T // L, Tc // L
    assert nt % SCAN_BODY_CHUNKS == 0 and SCAN_BODY_CHUNKS % 2 == 0
    row = lax.broadcasted_iota(jnp.int32, (L, L), 0)
    col = lax.broadcasted_iota(jnp.int32, (L, L), 1)
    lane = lax.broadcasted_iota(jnp.int32, (L, LANES), 1)
    krow = lax.broadcasted_iota(jnp.int32, (LANES, 1), 0)
    ones_ext = jnp.ones((L, LANES), BF16)
    R_R, R_LF, R_W, R_M, R_DEC = (k * hps for k in range(5))

    for d in (0, 1):
        tri = (col <= row) if d == 0 else (col >= row)
        trif_t = ((col >= row) if d == 0 else (col <= row)).astype(F32)
        c_ref[...] = jnp.zeros(c_ref.shape, F32)
        m_ref[...] = jnp.zeros(m_ref.shape, F32)

        def gate_body(n, carry, gref, slot0, d=d, trif_t=trif_t):
            GT = gref[pl.ds(pl.multiple_of(n * L, L), L), :].T[:4 * hps, :]
            lf = jax.nn.log_sigmoid(GT[(2 + d) * hps:(3 + d) * hps, :])
            b = sum(_dot(part.astype(F32), trif_t) for part in _split3(lf))
            r = GT[d * hps:(d + 1) * hps, :] - b
            b_last = b[:, L - 1:L] if d == 0 else b[:, 0:1]
            wlog = b_last + r
            wide = lambda x: jnp.broadcast_to(x, (hps, L))
            rows_ref[slot0 + n] = jnp.concatenate(
                [r, lf, wlog, wide(b_last), wide(jnp.max(wlog, axis=1, keepdims=True)),
                 jnp.zeros((rows_ref.shape[1] - 5 * hps, L), F32)], axis=0)
            return carry

        def stab_body(n, carry, nchunks, slot0, d=d):
            idx = slot0 + (n if d == 0 else nchunks - 1 - n)
            R = rows_ref[idx]
            m_prev = m_ref[0:hps, 0:1]
            bm = R[R_M:R_M + hps, 0:1] + m_prev
            m_new = jnp.maximum(bm, R[R_DEC:R_DEC + hps, 0:1])
            wide = lambda x: jnp.broadcast_to(x, (hps, L))
            rows_ref[idx, R_W:R_W + 3 * hps, :] = jnp.concatenate(
                [jnp.exp(R[R_W:R_W + hps, :] - m_new), wide(m_prev), wide(jnp.exp(bm - m_new))], axis=0)
            m_ref[0:hps, :] = jnp.broadcast_to(m_new, (hps, LANES))
            return carry

        lax.fori_loop(0, nct, functools.partial(gate_body, gref=gc_ref, slot0=0), 0, unroll=True)
        lax.fori_loop(0, nt, functools.partial(gate_body, gref=g_ref, slot0=nct), 0, unroll=8)
        lax.fori_loop(0, nct, functools.partial(stab_body, nchunks=nct, slot0=0), 0)
        lax.fori_loop(0, nt, functools.partial(stab_body, nchunks=nt, slot0=nct), 0, unroll=4)

        def update_state(p, R, rows, ktref, vref):
            kT = ktref[p * LANES:(p + 1) * LANES, rows].astype(F32)
            upd = None
            decay_rows = None
            for a in range(2):
                i = 2 * p + a
                vext = jnp.concatenate([vref[rows, i * LANES:(i + 1) * LANES], ones_ext], axis=1)
                rowmask = (krow < M_QK_DIM) if a == 0 else (krow >= M_QK_DIM)
                kw = jnp.where(rowmask, kT * R[R_W + i:R_W + i + 1, :], 0.0)
                u = _dot(kw.astype(BF16).astype(F32), vext.astype(F32))
                upd = u if upd is None else upd + u
                dr = jnp.where(rowmask, R[R_DEC + i:R_DEC + i + 1, 0:1], 0.0)
                decay_rows = dr if decay_rows is None else decay_rows + dr
            c_ref[p] = decay_rows * c_ref[p] + upd

        def ctx_body(n, carry, d=d, update_state=update_state):
            c = n if d == 0 else nct - 1 - n
            rows = pl.ds(pl.multiple_of(c * L, L), L)
            for p in range(hps // 2):
                update_state(p, rows_ref[c], rows, ktc_ref, vc_ref)
            return carry

        lax.fori_loop(0, nct, ctx_body, 0)

        def head_q(rows, p, a):
            q2 = q_ref[rows, p * LANES:(p + 1) * LANES]
            headmask = (lane < M_QK_DIM) if a == 0 else (lane >= M_QK_DIM)
            return jnp.where(headmask, q2, jnp.zeros_like(q2))

        def pair_weights(c, s_ref, wi_ref, fl_ref, tri=tri, head_q=head_q):
            rows = pl.ds(pl.multiple_of(c * L, L), L)
            R = rows_ref[nct + c]
            for i in range(hps):
                p, a = divmod(i, 2)
                s_raw = _dot(head_q(rows, p, a), kt_ref[p * LANES:(p + 1) * LANES, rows])
                m_prev = R[R_M + i:R_M + i + 1, 0:1]
                rm = jnp.where(tri, R[R_R + i:R_R + i + 1, :], -jnp.inf)
                mu = jnp.maximum(m_prev, jnp.max(rm, axis=1, keepdims=True))
                bcol = jnp.sum(jnp.where(tri, R[R_LF + i:R_LF + i + 1, :], 0.0), axis=1, keepdims=True)
                mu_b = jnp.broadcast_to(mu, (L, L))
                s_ref[i] = s_raw * jnp.exp(rm - mu_b)
                wi_ref[i] = jnp.exp(m_prev - mu_b)
                fl_ref[i] = jnp.exp(-jnp.broadcast_to(bcol + mu, (L, LANES)))

        def emit(c, s_ref, wi_ref, fl_ref, d=d, head_q=head_q, update_state=update_state):
            rows = pl.ds(pl.multiple_of(c * L, L), L)
            for p in range(hps // 2):
                C2b = c_ref[p].astype(BF16)
                for a in range(2):
                    i = 2 * p + a
                    cs = slice(i * LANES, (i + 1) * LANES)
                    vext = jnp.concatenate([v_ref[rows, cs], ones_ext], axis=1)
                    nd = (jnp.concatenate([wi_ref[i]] * 2, axis=1) * _dot(head_q(rows, p, a), C2b)
                          + _dot(s_ref[i].astype(BF16), vext))
                    h = nd[:, :LANES] / jnp.maximum(jnp.abs(nd[:, LANES:]), fl_ref[i])
                    if d == 0:
                        hacc_ref[rows, cs] = h
                    else:
                        out_ref[rows, cs] = (hacc_ref[rows, cs] + h).astype(BF16)
                update_state(p, rows_ref[nct + c], rows, kt_ref, v_ref)

        order = (lambda n: n) if d == 0 else (lambda n: nt - 1 - n)
        slot0 = (s0_ref, wi0_ref, fl0_ref)
        slot1 = (s1_ref, wi1_ref, fl1_ref)

        def lat_body(n, carry, pair_weights=pair_weights, emit=emit, order=order):
            slots = (slot0, slot1)
            for k in range(SCAN_BODY_CHUNKS):
                c = SCAN_BODY_CHUNKS * n + k
                pair_weights(order(jnp.minimum(c + 1, nt - 1)), *slots[(k + 1) % 2])
                emit(order(c), *slots[k % 2])
            return carry

        pair_weights(order(0), *slot0)
        lax.fori_loop(0, nt // SCAN_BODY_CHUNKS, lat_body, 0)


def _mlstm(q, kt, v, g, ktc, vc, gc):
    B, T, _ = q.shape
    Tc = vc.shape[1]
    hps = HEADS_PER_STEP
    G = M_HEADS // hps
    wq, wv = hps * M_QK_DIM, hps * M_V_DIM
    lat = lambda w: pl.BlockSpec((None, T, w), lambda b, g: (b, 0, g))
    cx = lambda w: pl.BlockSpec((None, Tc, w), lambda b, g: (b, 0, g))
    return pl.pallas_call(
        _mlstm_kernel,
        out_shape=jax.ShapeDtypeStruct((B, T, M_HEADS * M_V_DIM), BF16),
        grid=(B, G),
        in_specs=[lat(wq), pl.BlockSpec((wq, T), lambda b, g: (g, b)), lat(wv), lat(LANES),
                  pl.BlockSpec((wq, Tc), lambda b, g: (g, b)), cx(wv), cx(LANES)],
        out_specs=lat(wv),
        scratch_shapes=[pltpu.VMEM((T, wv), F32),
                        pltpu.VMEM((hps // 2, 2 * M_QK_DIM, 2 * LANES), F32),
                        pltpu.VMEM((8, LANES), F32),
                        pltpu.VMEM(((T + Tc) // M_CHUNK, 8 * hps, M_CHUNK), F32)]
                       + [pltpu.VMEM((hps, M_CHUNK, M_CHUNK), F32)] * 6,
        compiler_params=_params(("parallel", "parallel")),
        name="mlstm_scan",
    )(q, kt, v, g, ktc, vc, gc)


def _pivot_rows(b, h, off):
    L, N = b.shape
    bc = lambda r, n: jnp.broadcast_to(b[r:r + 1, :], (n, N))
    if 2 * h >= 8:
        return jnp.concatenate([bc(x + off, 2 * h) for x in range(0, L, 2 * h)], axis=0)
    sub = lax.broadcasted_iota(jnp.int32, (8, N), 0)
    tiles = []
    for x in range(0, L, 8):
        t = bc(x + 8 - 2 * h + off, 8)
        for y in range(8 - 4 * h, -1, -2 * h):
            t = jnp.where(sub < y + 2 * h, bc(x + y + off, 8), t)
        tiles.append(t)
    return jnp.concatenate(tiles, axis=0)


def _block_entry_rows(b, beta, d):
    L, N = b.shape
    tiles = []
    for x in range(0, L, beta):
        r = x - 1 if d == 0 else x + beta
        inside = 0 <= r < L
        tiles.append(jnp.broadcast_to(b[r:r + 1, :], (beta, N)) if inside else jnp.zeros((beta, N), F32))
    return jnp.concatenate(tiles, axis=0)


def _hgrn_kernel(q_ref, kf_ref, kb_ref, v_ref, lff_ref, lfb_ref,
                 kfc_ref, kbc_ref, vc_ref, lffc_ref, lfbc_ref,
                 out_ref, hacc_ref, st_ref, b_ref, bc_ref, attn0_ref, attn1_ref, *, beta):
    L = H_CHUNK
    hps = HEADS_PER_STEP
    T = q_ref.shape[0]
    Tc = vc_ref.shape[0]
    nt, nct = T // L, Tc // L
    assert nt % SCAN_BODY_CHUNKS == 0 and SCAN_BODY_CHUNKS % 2 == 0
    row = lax.broadcasted_iota(jnp.int32, (L, L), 0)
    col = lax.broadcasted_iota(jnp.int32, (L, L), 1)
    halves = [h for h in (L >> (n + 1) for n in range(L.bit_length() - 1)) if h >= beta]
    zpad_f = jnp.zeros((H_DIM - L, H_DIM), F32)
    zpad_b = jnp.zeros((H_DIM - L, H_DIM), BF16)
    neg_log2e = -1.4426950408889634

    for d in (0, 1):
        tri = (col <= row) if d == 0 else (col >= row)
        trib = tri.astype(BF16)
        last = L - 1 if d == 0 else 0
        st_ref[...] = jnp.zeros(st_ref.shape, F32)
        k_lat, lf_lat = (kf_ref, lff_ref) if d == 0 else (kb_ref, lfb_ref)
        k_ctx, lf_ctx = (kfc_ref, lffc_ref) if d == 0 else (kbc_ref, lfbc_ref)
        local = tri & ((row & -beta) == (col & -beta))
        pair = []
        for h in halves:
            same = (row & -(2 * h)) == (col & -(2 * h))
            r_late = (row & (2 * h - 1)) >= h
            c_late = (col & (2 * h - 1)) >= h
            pair.append(same & (r_late & ~c_late if d == 0 else ~r_late & c_late))

        def cumsum_body(n, carry, lfref, bref, trib=trib):
            rows = pl.ds(pl.multiple_of(n * L, L), L)
            bref[rows, :] = _tri_cumsum(trib, lfref[rows, :])
            return carry

        lax.fori_loop(0, nct, functools.partial(cumsum_body, lfref=lf_ctx, bref=bc_ref), 0, unroll=True)
        lax.fori_loop(0, nt, functools.partial(cumsum_body, lfref=lf_lat, bref=b_ref), 0, unroll=8)

        def update_state(i, rows, cs, kref, vref, bref):
            b_h = bref[rows, cs]
            bl = b_h[last:last + 1, :]
            kd = (kref[rows, cs].astype(F32) * jnp.exp(bl - b_h)).astype(BF16)
            vT = jnp.concatenate([vref[rows, cs].astype(F32), zpad_f], axis=0).T.astype(BF16)
            st_ref[i] = (st_ref[i] * jnp.exp(bl)
                         + _dot(vT, jnp.concatenate([kd, zpad_b], axis=0)))

        def ctx_body(n, carry, d=d, k_ctx=k_ctx, update_state=update_state):
            c = n if d == 0 else nct - 1 - n
            rows = pl.ds(pl.multiple_of(c * L, L), L)
            for i in range(hps):
                update_state(i, rows, slice(i * H_DIM, (i + 1) * H_DIM), k_ctx, vc_ref, bc_ref)
            return carry

        lax.fori_loop(0, nct, ctx_body, 0)

        def pair_weights(c, attn_ref, d=d, k_lat=k_lat, local=local, pair=pair):
            rows = pl.ds(pl.multiple_of(c * L, L), L)
            for i in range(hps):
                cs = slice(i * H_DIM, (i + 1) * H_DIM)
                b_h = b_ref[rows, cs]
                q_h = q_ref[rows, cs].astype(F32)
                k_h = k_lat[rows, cs].astype(F32)
                if beta > 1:
                    dl = (b_h - _block_entry_rows(b_h, beta, d)) * neg_log2e
                    qs = (q_h * jnp.exp2(-dl)).astype(BF16)
                    ks = (k_h * jnp.exp2(dl)).astype(BF16)
                else:
                    qs, ks = q_ref[rows, cs], k_lat[rows, cs]
                attn = jnp.where(local, lax.dot_general(qs, ks, _NT, preferred_element_type=F32), 0.0)
                for n, h in enumerate(halves):
                    z = jnp.exp2(jnp.abs(b_h - _pivot_rows(b_h, h, h - 1 + d)) * neg_log2e)
                    qk = lax.dot_general((q_h * z).astype(BF16), (k_h * z).astype(BF16), _NT,
                                         preferred_element_type=F32)
                    attn = jnp.where(pair[n], qk, attn)
                attn_ref[i] = attn

        def emit(c, attn_ref, d=d, k_lat=k_lat, update_state=update_state):
            rows = pl.ds(pl.multiple_of(c * L, L), L)
            for i in range(hps):
                cs = slice(i * H_DIM, (i + 1) * H_DIM)
                qc = (q_ref[rows, cs].astype(F32) * jnp.exp(b_ref[rows, cs])).astype(BF16)
                o = (lax.dot_general(qc, st_ref[i].astype(BF16), _NT, preferred_element_type=F32)
                     + _dot(attn_ref[i].astype(BF16), v_ref[rows, cs]))
                if d == 0:
                    hacc_ref[rows, cs] = o
                else:
                    out_ref[rows, cs] = (hacc_ref[rows, cs] + o).astype(BF16)
                update_state(i, rows, cs, k_lat, v_ref, b_ref)

        order = (lambda n: n) if d == 0 else (lambda n: nt - 1 - n)

        def lat_body(n, carry, pair_weights=pair_weights, emit=emit, order=order):
            slots = (attn0_ref, attn1_ref)
            for k in range(SCAN_BODY_CHUNKS):
                c = SCAN_BODY_CHUNKS * n + k
                pair_weights(order(jnp.minimum(c + 1, nt - 1)), slots[(k + 1) % 2])
                emit(order(c), slots[k % 2])
            return carry

        pair_weights(order(0), attn0_ref)
        lax.fori_loop(0, nt // SCAN_BODY_CHUNKS, lat_body, 0)


def _hgrn(beta, q, kf, kb, v, lff, lfb, kfc, kbc, vc, lffc, lfbc):
    B, T, W = q.shape
    Tc = vc.shape[1]
    hps = HEADS_PER_STEP
    G = H_HEADS // hps
    w = hps * H_DIM
    lat = pl.BlockSpec((None, T, w), lambda b, g: (b, 0, g))
    cx = pl.BlockSpec((None, Tc, w), lambda b, g: (b, 0, g))
    return pl.pallas_call(
        functools.partial(_hgrn_kernel, beta=beta),
        out_shape=jax.ShapeDtypeStruct((B, T, W), BF16),
        grid=(B, G),
        in_specs=[lat] * 6 + [cx] * 5,
        out_specs=lat,
        scratch_shapes=[pltpu.VMEM((T, w), F32),
                        pltpu.VMEM((hps, H_DIM, H_DIM), F32),
                        pltpu.VMEM((T, w), F32),
                        pltpu.VMEM((Tc, w), F32),
                        pltpu.VMEM((hps, H_CHUNK, H_CHUNK), F32),
                        pltpu.VMEM((hps, H_CHUNK, H_CHUNK), F32)],
        compiler_params=_params(("parallel", "parallel")),
        name="hgrn_scan_b%d" % beta,
    )(q, kf, kb, v, lff, lfb, kfc, kbc, vc, lffc, lfbc)


def _to_col_major(a):
    B, T, C = a.shape
    return a.reshape(B, T // GRID_W, GRID_W, C).transpose(0, 2, 1, 3).reshape(B, T, C)


def _to_row_major(a):
    B, T, C = a.shape
    return a.reshape(B, GRID_W, T // GRID_W, C).transpose(0, 2, 1, 3).reshape(B, T, C)


def kernel(x, c, ctx, c_ctx, ada_w, ada_b, ffn1_norm, ffn1_w_in, ffn1_w_out, mix_norm,
           mix_w_in, mix_b_in, mlstm_norm, hgrn_lb_logits, hgrn_norm, proj_m, proj_h,
           mix_w_out, ffn2_norm, ffn2_w_in, ffn2_w_out, final_norm):
    B, T, D = x.shape
    Tc = ctx.shape[1]
    hps = HEADS_PER_STEP
    MQ, MV, HW = M_HEADS * M_QK_DIM, M_HEADS * M_V_DIM, H_HEADS * H_DIM

    pad = (-(B + 1)) % 8
    cc = jnp.concatenate([c, c_ctx[None, :], jnp.zeros((pad, D), F32)], axis=0)
    mod = _modulation(cc, ada_w[0], ada_b[0][None, :])
    ml = mod[:B].reshape(B, N_MOD, D)
    mc = mod[B:B + 1].reshape(1, N_MOD, D)

    row = lambda v: v.reshape(1, -1).astype(F32)
    w1_in = ffn1_w_in[0].astype(BF16)
    w1_out = ffn1_w_out[0].astype(BF16)
    w2_in = ffn2_w_in[0].astype(BF16)
    w2_out = ffn2_w_out[0].astype(BF16)

    W, bias = mix_w_in[0], mix_b_in[0]
    o = 0
    src = {}
    for name, wd in (("mq", MQ), ("mk", MQ), ("mv", MV), ("mo", MV), ("ig", 2 * M_HEADS),
                     ("fg", 2 * M_HEADS), ("hq", HW), ("hff", HW), ("hfb", HW), ("hi", HW),
                     ("hg", HW), ("gm", D), ("gh", D)):
        src[name] = (o, o + wd)
        o += wd
    ws, bs, col, o = [], [], {}, 0
    for name in ("mq", "mk", "mv", "mo", "hq", "hff", "hfb", "hi", "hg", "gm", "gh"):
        a, b = src[name]
        scale = M_QK_DIM ** -0.5 if name == "mq" else 1.0
        ws.append(W[:, a:b] * scale)
        bs.append(bias[a:b] * scale)
        col[name] = o
        o += b - a
    col["gates"] = o
    for g in range(M_HEADS // hps):
        for n in ("ig", "fg"):
            for off in (0, M_HEADS):
                a = src[n][0] + off + g * hps
                ws.append(W[:, a:a + hps])
                bs.append(bias[a:a + hps])
        ws.append(jnp.zeros((D, LANES - 4 * hps), F32))
        bs.append(jnp.zeros((LANES - 4 * hps,), F32))
    w_mix = jnp.concatenate(ws, axis=1).astype(BF16)
    b_mix = jnp.concatenate(bs)[None, :]

    n_gate = LANES * (M_HEADS // hps)
    lb = jnp.cumsum(jax.nn.softmax(hgrn_lb_logits.astype(F32), axis=1), axis=1)[:, 0]

    x2d = x.reshape(B * T, D)
    c2d = ctx.reshape(B * Tc, D)
    tpb = T // TOKEN_TILE
    (xm_c,) = _ffn(c2d, mc, None, row(ffn1_norm[0]), w1_in, w1_out, row(mix_norm[0]),
                   rows=(0, 1, 2), rows2=(3, 4), out_x=False, aux="mix")
    x1, xm_l = _ffn(x2d, ml, tpb, row(ffn1_norm[0]), w1_in, w1_out, row(mix_norm[0]),
                    rows=(0, 1, 2), rows2=(3, 4), out_x=True, aux="mix")

    seg = lambda name, wd, kind, arg=0: (col[name], wd, kind, arg)
    mq, mk, mv, mso, mg, sgm, sgh, hsg = _proj(
        xm_l, w_mix, b_mix, lb,
        [seg("mq", MQ, "bf16"), seg("mk", MQ, "bf16_t"), seg("mv", MV, "bf16"),
         seg("mo", MV, "sigmoid"), seg("gates", n_gate, "f32"), seg("gm", D, "sigmoid"),
         seg("gh", D, "sigmoid"), seg("hg", HW, "silu")], "proj_mlstm")
    mkc, mvc, mgc = _proj(
        xm_c, w_mix, b_mix, lb,
        [seg("mk", MQ, "bf16_t"), seg("mv", MV, "bf16"), seg("gates", n_gate, "f32")],
        "proj_mlstm_ctx")

    xm_cm = _to_col_major(xm_l.reshape(B, T, D)).reshape(B * T, D)
    hq, lff, hkf, lfb, hkb, hv = _proj(
        xm_cm, w_mix, b_mix, lb,
        [seg("hq", HW, "silu"), seg("hff", HW, "hgrn", 0), seg("hfb", HW, "hgrn", 1),
         seg("hi", HW, "bf16")], "proj_hgrn")
    lffc, hkfc, lfbc, hkbc, hvc = _proj(
        xm_c, w_mix, b_mix, lb,
        [seg("hff", HW, "hgrn", 0), seg("hfb", HW, "hgrn", 1), seg("hi", HW, "bf16")],
        "proj_hgrn_ctx")

    r3 = lambda a, t: a.reshape(B, t, a.shape[-1])
    hm = _mlstm(r3(mq, T), mk, r3(mv, T), r3(mg, T), mkc, r3(mvc, Tc), r3(mgc, Tc))
    hgrn_args = (r3(hq, T), r3(hkf, T), r3(hkb, T), r3(hv, T), r3(lff, T), r3(lfb, T),
                 r3(hkfc, Tc), r3(hkbc, Tc), r3(hvc, Tc), r3(lffc, Tc), r3(lfbc, Tc))
    block_ok = H_BLOCK * jnp.max(-jnp.log(lb)) <= H_BLOCK_MAX_EXPONENT
    hh_cm = lax.cond(block_ok, functools.partial(_hgrn, H_BLOCK), functools.partial(_hgrn, 1),
                     *hgrn_args)
    hh = _to_row_major(hh_cm)

    merge = (hm.reshape(B * T, MV), hh.reshape(B * T, HW), mso, hsg, sgm, sgh,
             row(mlstm_norm[0]), row(hgrn_norm[0]),
             proj_m[0].astype(BF16), proj_h[0].astype(BF16), mix_w_out[0].astype(BF16), 5)
    (out,) = _ffn(x1, ml, tpb, row(ffn2_norm[0]), w2_in, w2_out, row(final_norm),
                  rows=(6, 7, 8), rows2=(3, 4), out_x=False, aux="final", merge=merge)
    return out.reshape(B, T, D)
```

```python
import functools

import jax
import jax.numpy as jnp
from jax import lax
from jax.experimental import pallas as pl
from jax.experimental.pallas import tpu as pltpu

F32 = jnp.float32
BF16 = jnp.bfloat16
EPS = 1e-6

GRID_W = 64
M_HEADS = 8
M_QK_DIM = 64
M_V_DIM = 128
H_HEADS = 8
H_DIM = 128
N_MOD = 9

LANES = 128
TOKEN_TILE = 512
FFN_CHUNK = 256
PROJ_CHUNK = 512
M_CHUNK = 128
H_CHUNK = 64
H_BLOCK = 16
H_BLOCK_MAX_EXPONENT = 64.0
HEADS_PER_STEP = 4
SCAN_BODY_CHUNKS = 8
VMEM_LIMIT = 56 * 1024 * 1024

_NT = (((1,), (1,)), ((), ()))


def _params(sem, vmem=VMEM_LIMIT):
    return pltpu.CompilerParams(dimension_semantics=sem, vmem_limit_bytes=vmem)


def _resident(shape):
    nd = len(shape)
    return pl.BlockSpec(shape, lambda *_: (0,) * nd, pipeline_mode=pl.Buffered(1))


def _dot(a, b):
    return jnp.dot(a, b, preferred_element_type=F32)


def _rms(x, g):
    ms = jnp.mean(x * x, axis=-1, keepdims=True)
    return x * lax.rsqrt(ms + EPS) * g


def _split3(x):
    hi = x.astype(BF16)
    r = x - hi.astype(F32)
    mid = r.astype(BF16)
    lo = (r - mid.astype(F32)).astype(BF16)
    return hi, mid, lo


def _tri_cumsum(tri, x):
    hi, mid, _ = _split3(x)
    return _dot(tri, hi) + _dot(tri, mid)


def _mod_kernel(c_ref, w_ref, b_ref, o_ref):
    c = c_ref[...]
    a = c * jax.nn.sigmoid(c)
    a_hi = a.astype(BF16)
    a_lo = (a - a_hi.astype(F32)).astype(BF16)
    w = w_ref[...]
    w_hi = w.astype(BF16)
    w_lo = (w - w_hi.astype(F32)).astype(BF16)
    o_ref[...] = _dot(a_hi, w_hi) + _dot(a_hi, w_lo) + _dot(a_lo, w_hi) + b_ref[...]


def _modulation(cc, w, b):
    R, D = cc.shape
    N = w.shape[1]
    tn = 1024
    return pl.pallas_call(
        _mod_kernel,
        out_shape=jax.ShapeDtypeStruct((R, N), F32),
        grid=(N // tn,),
        in_specs=[pl.BlockSpec((R, D), lambda j: (0, 0)),
                  pl.BlockSpec((D, tn), lambda j: (0, j)),
                  pl.BlockSpec((1, tn), lambda j: (0, j))],
        out_specs=pl.BlockSpec((R, tn), lambda j: (0, j)),
        compiler_params=_params(("arbitrary",)),
        name="modulation",
    )(cc, w, b)


def _ffn_kernel(x_ref, mod_ref, g_ref, win_ref, wout_ref, g2_ref, *rest,
                rows, rows2, out_x, aux, n_chunks, merge_gate_row):
    fc = FFN_CHUNK
    x = x_ref[...]
    mod = mod_ref[0]
    outs = rest
    if merge_gate_row is not None:
        (hm_ref, hh_ref, om_ref, oh_ref, gm_ref, gh_ref, nm_ref, nh_ref,
         pm_ref, ph_ref, wo_ref) = rest[:11]
        outs = rest[11:]

        def branch(h_ref, gate_ref, gain_ref):
            heads = []
            for j in range(h_ref.shape[1] // LANES):
                cs = slice(j * LANES, (j + 1) * LANES)
                heads.append(_rms(h_ref[:, cs].astype(F32), gain_ref[:, cs])
                             * gate_ref[:, cs].astype(F32))
            return jnp.concatenate(heads, axis=1).astype(BF16)

        z = (gm_ref[...].astype(F32) * _dot(branch(hm_ref, om_ref, nm_ref), pm_ref[...])
             + gh_ref[...].astype(F32) * _dot(branch(hh_ref, oh_ref, nh_ref), ph_ref[...])
             ).astype(BF16)
        x = x + mod[merge_gate_row:merge_gate_row + 1] * _dot(z, wo_ref[...])
    shift = mod[rows[0]:rows[0] + 1]
    scale = mod[rows[1]:rows[1] + 1]
    gate = mod[rows[2]:rows[2] + 1]
    xm = (_rms(x, g_ref[...]) * (1.0 + scale) + shift).astype(BF16)
    acc = jnp.zeros(x.shape, F32)
    F = n_chunks * fc
    for j in range(n_chunks):
        a = _dot(xm, win_ref[:, j * fc:(j + 1) * fc])
        u = _dot(xm, win_ref[:, F + j * fc:F + (j + 1) * fc])
        h = (a * jax.nn.sigmoid(a) * u).astype(BF16)
        acc = acc + _dot(h, wout_ref[j * fc:(j + 1) * fc, :])
    xn = x + 0.5 * gate * acc
    k = 0
    if out_x:
        outs[k][...] = xn
        k += 1
    if aux == "mix":
        shift2 = mod[rows2[0]:rows2[0] + 1]
        scale2 = mod[rows2[1]:rows2[1] + 1]
        outs[k][...] = (_rms(xn, g2_ref[...]) * (1.0 + scale2) + shift2).astype(BF16)
    elif aux == "final":
        outs[k][...] = _rms(xn, g2_ref[...])


def _ffn(x2d, mod, tiles_per_mod, g, win, wout, g2, *, rows, rows2, out_x, aux, merge=None):
    N, D = x2d.shape
    F = wout.shape[0]
    tm = TOKEN_TILE
    out_shape, out_specs = [], []
    if out_x:
        out_shape.append(jax.ShapeDtypeStruct((N, D), F32))
        out_specs.append(pl.BlockSpec((tm, D), lambda i: (i, 0)))
    if aux == "mix":
        out_shape.append(jax.ShapeDtypeStruct((N, D), BF16))
        out_specs.append(pl.BlockSpec((tm, D), lambda i: (i, 0)))
    elif aux == "final":
        out_shape.append(jax.ShapeDtypeStruct((N, D), F32))
        out_specs.append(pl.BlockSpec((tm, D), lambda i: (i, 0)))
    if tiles_per_mod is None:
        mod_map = lambda i: (0, 0, 0)
    else:
        mod_map = lambda i: (i // tiles_per_mod, 0, 0)
    kern = functools.partial(_ffn_kernel, rows=rows, rows2=rows2, out_x=out_x, aux=aux,
                             n_chunks=F // FFN_CHUNK,
                             merge_gate_row=None if merge is None else merge[11])
    tile = pl.BlockSpec((tm, D), lambda i: (i, 0))
    in_specs = [tile, pl.BlockSpec((1, N_MOD, D), mod_map), _resident((1, D)),
                _resident(win.shape), _resident(wout.shape), _resident((1, D))]
    args = [x2d, mod, g, win, wout, g2]
    if merge is not None:
        in_specs += [tile] * 6 + [_resident(w.shape) for w in merge[6:11]]
        args += list(merge[:11])
    return pl.pallas_call(
        kern,
        out_shape=out_shape,
        grid=(N // tm,),
        in_specs=in_specs,
        out_specs=out_specs,
        compiler_params=_params(("parallel",)),
        name=("merge_" if merge is not None else "") + "ffn_" + aux,
    )(*args)


def _proj_kernel(x_ref, w_ref, b_ref, lb_ref, *outs, segs):
    x = x_ref[...]
    oi = 0
    for c0, wd, kind, arg in segs:
        for s in range(0, wd, PROJ_CHUNK):
            w = min(PROJ_CHUNK, wd - s)
            y = _dot(x, w_ref[:, c0 + s:c0 + s + w]) + b_ref[:, c0 + s:c0 + s + w]
            if kind == "bf16":
                outs[oi][:, s:s + w] = y.astype(BF16)
            elif kind == "bf16_t":
                outs[oi][s:s + w, :] = y.T.astype(BF16)
            elif kind == "f32":
                outs[oi][:, s:s + w] = y
            elif kind == "sigmoid":
                outs[oi][:, s:s + w] = jax.nn.sigmoid(y).astype(BF16)
            elif kind == "silu":
                outs[oi][:, s:s + w] = (y * jax.nn.sigmoid(y)).astype(BF16)
            elif kind == "hgrn":
                lb = lb_ref[arg:arg + 1, s:s + w]
                sg = jax.nn.sigmoid(y)
                outs[oi][:, s:s + w] = jnp.log(lb + (1.0 - lb) * sg)
                outs[oi + 1][:, s:s + w] = ((1.0 - lb) * (1.0 - sg)).astype(BF16)
        oi += 2 if kind == "hgrn" else 1


def _proj(x2d, w, b, lb, segs, name):
    N, D = x2d.shape
    tm = TOKEN_TILE
    out_shape, out_specs = [], []
    for _, wd, kind, _ in segs:
        if kind == "bf16_t":
            out_shape.append(jax.ShapeDtypeStruct((wd, N), BF16))
            out_specs.append(pl.BlockSpec((wd, tm), lambda i: (0, i)))
            continue
        dts = {"bf16": [BF16], "f32": [F32], "sigmoid": [BF16], "silu": [BF16],
               "hgrn": [F32, BF16]}[kind]
        for dt in dts:
            out_shape.append(jax.ShapeDtypeStruct((N, wd), dt))
            out_specs.append(pl.BlockSpec((tm, wd), lambda i: (i, 0)))
    return pl.pallas_call(
        functools.partial(_proj_kernel, segs=tuple(segs)),
        out_shape=out_shape,
        grid=(N // tm,),
        in_specs=[pl.BlockSpec((tm, D), lambda i: (i, 0)),
                  _resident(w.shape),
                  _resident(b.shape),
                  _resident(lb.shape)],
        out_specs=out_specs,
        compiler_params=_params(("parallel",)),
        name=name,
    )(x2d, w, b, lb)


def _mlstm_kernel(q_ref, kt_ref, v_ref, g_ref, ktc_ref, vc_ref, gc_ref,
                  out_ref, hacc_ref, c_ref, m_ref, rows_ref,
                  s0_ref, wi0_ref, fl0_ref, s1_ref, wi1_ref, fl1_ref):
    L = M_CHUNK
    hps = HEADS_PER_STEP
    T = q_ref.shape[0]
    Tc = vc_ref.shape[0]
    nt, nct = T // L, Tc // L
    assert nt % SCAN_BODY_CHUNKS == 0 and SCAN_BODY_CHUNKS % 2 == 0
    row = lax.broadcasted_iota(jnp.int32, (L, L), 0)
    col = lax.broadcasted_iota(jnp.int32, (L, L), 1)
    lane = lax.broadcasted_iota(jnp.int32, (L, LANES), 1)
    krow = lax.broadcasted_iota(jnp.int32, (LANES, 1), 0)
    ones_ext = jnp.ones((L, LANES), BF16)
    R_R, R_LF, R_W, R_M, R_DEC = (k * hps for k in range(5))

    for d in (0, 1):
        tri = (col <= row) if d == 0 else (col >= row)
        trif_t = ((col >= row) if d == 0 else (col <= row)).astype(F32)
        c_ref[...] = jnp.zeros(c_ref.shape, F32)
        m_ref[...] = jnp.zeros(m_ref.shape, F32)

        def gate_body(n, carry, gref, slot0, d=d, trif_t=trif_t):
            GT = gref[pl.ds(pl.multiple_of(n * L, L), L), :].T[:4 * hps, :]
            lf = jax.nn.log_sigmoid(GT[(2 + d) * hps:(3 + d) * hps, :])
            b = sum(_dot(part.astype(F32), trif_t) for part in _split3(lf))
            r = GT[d * hps:(d + 1) * hps, :] - b
            b_last = b[:, L - 1:L] if d == 0 else b[:, 0:1]
            wlog = b_last + r
            wide = lambda x: jnp.broadcast_to(x, (hps, L))
            rows_ref[slot0 + n] = jnp.concatenate(
                [r, lf, wlog, wide(b_last), wide(jnp.max(wlog, axis=1, keepdims=True)),
                 jnp.zeros((rows_ref.shape[1] - 5 * hps, L), F32)], axis=0)
            return carry

        def stab_body(n, carry, nchunks, slot0, d=d):
            idx = slot0 + (n if d == 0 else nchunks - 1 - n)
            R = rows_ref[idx]
            m_prev = m_ref[0:hps, 0:1]
            bm = R[R_M:R_M + hps, 0:1] + m_prev
            m_new = jnp.maximum(bm, R[R_DEC:R_DEC + hps, 0:1])
            wide = lambda x: jnp.broadcast_to(x, (hps, L))
            rows_ref[idx, R_W:R_W + 3 * hps, :] = jnp.concatenate(
                [jnp.exp(R[R_W:R_W + hps, :] - m_new), wide(m_prev), wide(jnp.exp(bm - m_new))], axis=0)
            m_ref[0:hps, :] = jnp.broadcast_to(m_new, (hps, LANES))
            return carry

        lax.fori_loop(0, nct, functools.partial(gate_body, gref=gc_ref, slot0=0), 0, unroll=True)
        lax.fori_loop(0, nt, functools.partial(gate_body, gref=g_ref, slot0=nct), 0, unroll=8)
        lax.fori_loop(0, nct, functools.partial(stab_body, nchunks=nct, slot0=0), 0)
        lax.fori_loop(0, nt, functools.partial(stab_body, nchunks=nt, slot0=nct), 0, unroll=4)

        def update_state(p, R, rows, ktref, vref):
            kT = ktref[p * LANES:(p + 1) * LANES, rows].astype(F32)
            upd = None
            decay_rows = None
            for a in range(2):
                i = 2 * p + a
                vext = jnp.concatenate([vref[rows, i * LANES:(i + 1) * LANES], ones_ext], axis=1)
                rowmask = (krow < M_QK_DIM) if a == 0 else (krow >= M_QK_DIM)
                kw = jnp.where(rowmask, kT * R[R_W + i:R_W + i + 1, :], 0.0)
                u = _dot(kw.astype(BF16).astype(F32), vext.astype(F32))
                upd = u if upd is None else upd + u
                dr = jnp.where(rowmask, R[R_DEC + i:R_DEC + i + 1, 0:1], 0.0)
                decay_rows = dr if decay_rows is None else decay_rows + dr
            c_ref[p] = decay_rows * c_ref[p] + upd

        def ctx_body(n, carry, d=d, update_state=update_state):
            c = n if d == 0 else nct - 1 - n
            rows = pl.ds(pl.multiple_of(c * L, L), L)
            for p in range(hps // 2):
                update_state(p, rows_ref[c], rows, ktc_ref, vc_ref)
            return carry

        lax.fori_loop(0, nct, ctx_body, 0)

        def head_q(rows, p, a):
            q2 = q_ref[rows, p * LANES:(p + 1) * LANES]
            headmask = (lane < M_QK_DIM) if a == 0 else (lane >= M_QK_DIM)
            return jnp.where(headmask, q2, jnp.zeros_like(q2))

        def pair_weights(c, s_ref, wi_ref, fl_ref, tri=tri, head_q=head_q):
            rows = pl.ds(pl.multiple_of(c * L, L), L)
            R = rows_ref[nct + c]
            for i in range(hps):
                p, a = divmod(i, 2)
                s_raw = _dot(head_q(rows, p, a), kt_ref[p * LANES:(p + 1) * LANES, rows])
                m_prev = R[R_M + i:R_M + i + 1, 0:1]
                rm = jnp.where(tri, R[R_R + i:R_R + i + 1, :], -jnp.inf)
                mu = jnp.maximum(m_prev, jnp.max(rm, axis=1, keepdims=True))
                bcol = jnp.sum(jnp.where(tri, R[R_LF + i:R_LF + i + 1, :], 0.0), axis=1, keepdims=True)
                mu_b = jnp.broadcast_to(mu, (L, L))
                s_ref[i] = s_raw * jnp.exp(rm - mu_b)
                wi_ref[i] = jnp.exp(m_prev - mu_b)
                fl_ref[i] = jnp.exp(-jnp.broadcast_to(bcol + mu, (L, LANES)))

        def emit(c, s_ref, wi_ref, fl_ref, d=d, head_q=head_q, update_state=update_state):
            rows = pl.ds(pl.multiple_of(c * L, L), L)
            for p in range(hps // 2):
                C2b = c_ref[p].astype(BF16)
                for a in range(2):
                    i = 2 * p + a
                    cs = slice(i * LANES, (i + 1) * LANES)
                    vext = jnp.concatenate([v_ref[rows, cs], ones_ext], axis=1)
                    nd = (jnp.concatenate([wi_ref[i]] * 2, axis=1) * _dot(head_q(rows, p, a), C2b)
                          + _dot(s_ref[i].astype(BF16), vext))
                    h = nd[:, :LANES] / jnp.maximum(jnp.abs(nd[:, LANES:]), fl_ref[i])
                    if d == 0:
                        hacc_ref[rows, cs] = h
                    else:
                        out_ref[rows, cs] = (hacc_ref[rows, cs] + h).astype(BF16)
                update_state(p, rows_ref[nct + c], rows, kt_ref, v_ref)

        order = (lambda n: n) if d == 0 else (lambda n: nt - 1 - n)
        slot0 = (s0_ref, wi0_ref, fl0_ref)
        slot1 = (s1_ref, wi1_ref, fl1_ref)

        def lat_body(n, carry, pair_weights=pair_weights, emit=emit, order=order):
            slots = (slot0, slot1)
            for k in range(SCAN_BODY_CHUNKS):
                c = SCAN_BODY_CHUNKS * n + k
                pair_weights(order(jnp.minimum(c + 1, nt - 1)), *slots[(k + 1) % 2])
                emit(order(c), *slots[k % 2])
            return carry

        pair_weights(order(0), *slot0)
        lax.fori_loop(0, nt // SCAN_BODY_CHUNKS, lat_body, 0)


def _mlstm(q, kt, v, g, ktc, vc, gc):
    B, T, _ = q.shape
    Tc = vc.shape[1]
    hps = HEADS_PER_STEP
    G = M_HEADS // hps
    wq, wv = hps * M_QK_DIM, hps * M_V_DIM
    lat = lambda w: pl.BlockSpec((None, T, w), lambda b, g: (b, 0, g))
    cx = lambda w: pl.BlockSpec((None, Tc, w), lambda b, g: (b, 0, g))
    return pl.pallas_call(
        _mlstm_kernel,
        out_shape=jax.ShapeDtypeStruct((B, T, M_HEADS * M_V_DIM), BF16),
        grid=(B, G),
        in_specs=[lat(wq), pl.BlockSpec((wq, T), lambda b, g: (g, b)), lat(wv), lat(LANES),
                  pl.BlockSpec((wq, Tc), lambda b, g: (g, b)), cx(wv), cx(LANES)],
        out_specs=lat(wv),
        scratch_shapes=[pltpu.VMEM((T, wv), F32),
                        pltpu.VMEM((hps // 2, 2 * M_QK_DIM, 2 * LANES), F32),
                        pltpu.VMEM((8, LANES), F32),
                        pltpu.VMEM(((T + Tc) // M_CHUNK, 8 * hps, M_CHUNK), F32)]
                       + [pltpu.VMEM((hps, M_CHUNK, M_CHUNK), F32)] * 6,
        compiler_params=_params(("parallel", "parallel")),
        name="mlstm_scan",
    )(q, kt, v, g, ktc, vc, gc)


def _pivot_rows(b, h, off):
    L, N = b.shape
    bc = lambda r, n: jnp.broadcast_to(b[r:r + 1, :], (n, N))
    if 2 * h >= 8:
        return jnp.concatenate([bc(x + off, 2 * h) for x in range(0, L, 2 * h)], axis=0)
    sub = lax.broadcasted_iota(jnp.int32, (8, N), 0)
    tiles = []
    for x in range(0, L, 8):
        t = bc(x + 8 - 2 * h + off, 8)
        for y in range(8 - 4 * h, -1, -2 * h):
            t = jnp.where(sub < y + 2 * h, bc(x + y + off, 8), t)
        tiles.append(t)
    return jnp.concatenate(tiles, axis=0)


def _block_entry_rows(b, beta, d):
    L, N = b.shape
    tiles = []
    for x in range(0, L, beta):
        r = x - 1 if d == 0 else x + beta
        inside = 0 <= r < L
        tiles.append(jnp.broadcast_to(b[r:r + 1, :], (beta, N)) if inside else jnp.zeros((beta, N), F32))
    return jnp.concatenate(tiles, axis=0)


def _hgrn_kernel(q_ref, kf_ref, kb_ref, v_ref, lff_ref, lfb_ref,
                 kfc_ref, kbc_ref, vc_ref, lffc_ref, lfbc_ref,
                 out_ref, hacc_ref, st_ref, b_ref, bc_ref, attn0_ref, attn1_ref, *, beta):
    L = H_CHUNK
    hps = HEADS_PER_STEP
    T = q_ref.shape[0]
    Tc = vc_ref.shape[0]
    nt, nct = T // L, Tc // L
    assert nt % SCAN_BODY_CHUNKS == 0 and SCAN_BODY_CHUNKS % 2 == 0
    row = lax.broadcasted_iota(jnp.int32, (L, L), 0)
    col = lax.broadcasted_iota(jnp.int32, (L, L), 1)
    halves = [h for h in (L >> (n + 1) for n in range(L.bit_length() - 1)) if h >= beta]
    zpad_f = jnp.zeros((H_DIM - L, H_DIM), F32)
    zpad_b = jnp.zeros((H_DIM - L, H_DIM), BF16)
    neg_log2e = -1.4426950408889634

    for d in (0, 1):
        tri = (col <= row) if d == 0 else (col >= row)
        trib = tri.astype(BF16)
        last = L - 1 if d == 0 else 0
        st_ref[...] = jnp.zeros(st_ref.shape, F32)
        k_lat, lf_lat = (kf_ref, lff_ref) if d == 0 else (kb_ref, lfb_ref)
        k_ctx, lf_ctx = (kfc_ref, lffc_ref) if d == 0 else (kbc_ref, lfbc_ref)
        local = tri & ((row & -beta) == (col & -beta))
        pair = []
        for h in halves:
            same = (row & -(2 * h)) == (col & -(2 * h))
            r_late = (row & (2 * h - 1)) >= h
            c_late = (col & (2 * h - 1)) >= h
            pair.append(same & (r_late & ~c_late if d == 0 else ~r_late & c_late))

        def cumsum_body(n, carry, lfref, bref, trib=trib):
            rows = pl.ds(pl.multiple_of(n * L, L), L)
            bref[rows, :] = _tri_cumsum(trib, lfref[rows, :])
            return carry

        lax.fori_loop(0, nct, functools.partial(cumsum_body, lfref=lf_ctx, bref=bc_ref), 0, unroll=True)
        lax.fori_loop(0, nt, functools.partial(cumsum_body, lfref=lf_lat, bref=b_ref), 0, unroll=8)

        def update_state(i, rows, cs, kref, vref, bref):
            b_h = bref[rows, cs]
            bl = b_h[last:last + 1, :]
            kd = (kref[rows, cs].astype(F32) * jnp.exp(bl - b_h)).astype(BF16)
            vT = jnp.concatenate([vref[rows, cs].astype(F32), zpad_f], axis=0).T.astype(BF16)
            st_ref[i] = (st_ref[i] * jnp.exp(bl)
                         + _dot(vT, jnp.concatenate([kd, zpad_b], axis=0)))

        def ctx_body(n, carry, d=d, k_ctx=k_ctx, update_state=update_state):
            c = n if d == 0 else nct - 1 - n
            rows = pl.ds(pl.multiple_of(c * L, L), L)
            for i in range(hps):
                update_state(i, rows, slice(i * H_DIM, (i + 1) * H_DIM), k_ctx, vc_ref, bc_ref)
            return carry

        lax.fori_loop(0, nct, ctx_body, 0)

        def pair_weights(c, attn_ref, d=d, k_lat=k_lat, local=local, pair=pair):
            rows = pl.ds(pl.multiple_of(c * L, L), L)
            for i in range(hps):
                cs = slice(i * H_DIM, (i + 1) * H_DIM)
                b_h = b_ref[rows, cs]
                q_h = q_ref[rows, cs].astype(F32)
                k_h = k_lat[rows, cs].astype(F32)
                if beta > 1:
                    dl = (b_h - _block_entry_rows(b_h, beta, d)) * neg_log2e
                    qs = (q_h * jnp.exp2(-dl)).astype(BF16)
                    ks = (k_h * jnp.exp2(dl)).astype(BF16)
                else:
                    qs, ks = q_ref[rows, cs], k_lat[rows, cs]
                attn = jnp.where(local, lax.dot_general(qs, ks, _NT, preferred_element_type=F32), 0.0)
                for n, h in enumerate(halves):
                    z = jnp.exp2(jnp.abs(b_h - _pivot_rows(b_h, h, h - 1 + d)) * neg_log2e)
                    qk = lax.dot_general((q_h * z).astype(BF16), (k_h * z).astype(BF16), _NT,
                                         preferred_element_type=F32)
                    attn = jnp.where(pair[n], qk, attn)
                attn_ref[i] = attn

        def emit(c, attn_ref, d=d, k_lat=k_lat, update_state=update_state):
            rows = pl.ds(pl.multiple_of(c * L, L), L)
            for i in range(hps):
                cs = slice(i * H_DIM, (i + 1) * H_DIM)
                qc = (q_ref[rows, cs].astype(F32) * jnp.exp(b_ref[rows, cs])).astype(BF16)
                o = (lax.dot_general(qc, st_ref[i].astype(BF16), _NT, preferred_element_type=F32)
                     + _dot(attn_ref[i].astype(BF16), v_ref[rows, cs]))
                if d == 0:
                    hacc_ref[rows, cs] = o
                else:
                    out_ref[rows, cs] = (hacc_ref[rows, cs] + o).astype(BF16)
                update_state(i, rows, cs, k_lat, v_ref, b_ref)

        order = (lambda n: n) if d == 0 else (lambda n: nt - 1 - n)

        def lat_body(n, carry, pair_weights=pair_weights, emit=emit, order=order):
            slots = (attn0_ref, attn1_ref)
            for k in range(SCAN_BODY_CHUNKS):
                c = SCAN_BODY_CHUNKS * n + k
                pair_weights(order(jnp.minimum(c + 1, nt - 1)), slots[(k + 1) % 2])
                emit(order(c), slots[k % 2])
            return carry

        pair_weights(order(0), attn0_ref)
        lax.fori_loop(0, nt // SCAN_BODY_CHUNKS, lat_body, 0)


def _hgrn(beta, q, kf, kb, v, lff, lfb, kfc, kbc, vc, lffc, lfbc):
    B, T, W = q.shape
    Tc = vc.shape[1]
    hps = HEADS_PER_STEP
    G = H_HEADS // hps
    w = hps * H_DIM
    lat = pl.BlockSpec((None, T, w), lambda b, g: (b, 0, g))
    cx = pl.BlockSpec((None, Tc, w), lambda b, g: (b, 0, g))
    return pl.pallas_call(
        functools.partial(_hgrn_kernel, beta=beta),
        out_shape=jax.ShapeDtypeStruct((B, T, W), BF16),
        grid=(B, G),
        in_specs=[lat] * 6 + [cx] * 5,
        out_specs=lat,
        scratch_shapes=[pltpu.VMEM((T, w), F32),
                        pltpu.VMEM((hps, H_DIM, H_DIM), F32),
                        pltpu.VMEM((T, w), F32),
                        pltpu.VMEM((Tc, w), F32),
                        pltpu.VMEM((hps, H_CHUNK, H_CHUNK), F32),
                        pltpu.VMEM((hps, H_CHUNK, H_CHUNK), F32)],
        compiler_params=_params(("parallel", "parallel")),
        name="hgrn_scan_b%d" % beta,
    )(q, kf, kb, v, lff, lfb, kfc, kbc, vc, lffc, lfbc)


def _to_col_major(a):
    B, T, C = a.shape
    return a.reshape(B, T // GRID_W, GRID_W, C).transpose(0, 2, 1, 3).reshape(B, T, C)


def _to_row_major(a):
    B, T, C = a.shape
    return a.reshape(B, GRID_W, T // GRID_W, C).transpose(0, 2, 1, 3).reshape(B, T, C)


def kernel(x, c, ctx, c_ctx, ada_w, ada_b, ffn1_norm, ffn1_w_in, ffn1_w_out, mix_norm,
           mix_w_in, mix_b_in, mlstm_norm, hgrn_lb_logits, hgrn_norm, proj_m, proj_h,
           mix_w_out, ffn2_norm, ffn2_w_in, ffn2_w_out, final_norm):
    B, T, D = x.shape
    Tc = ctx.shape[1]
    hps = HEADS_PER_STEP
    MQ, MV, HW = M_HEADS * M_QK_DIM, M_HEADS * M_V_DIM, H_HEADS * H_DIM

    pad = (-(B + 1)) % 8
    cc = jnp.concatenate([c, c_ctx[None, :], jnp.zeros((pad, D), F32)], axis=0)
    mod = _modulation(cc, ada_w[0], ada_b[0][None, :])
    ml = mod[:B].reshape(B, N_MOD, D)
    mc = mod[B:B + 1].reshape(1, N_MOD, D)

    row = lambda v: v.reshape(1, -1).astype(F32)
    w1_in = ffn1_w_in[0].astype(BF16)
    w1_out = ffn1_w_out[0].astype(BF16)
    w2_in = ffn2_w_in[0].astype(BF16)
    w2_out = ffn2_w_out[0].astype(BF16)

    W, bias = mix_w_in[0], mix_b_in[0]
    o = 0
    src = {}
    for name, wd in (("mq", MQ), ("mk", MQ), ("mv", MV), ("mo", MV), ("ig", 2 * M_HEADS),
                     ("fg", 2 * M_HEADS), ("hq", HW), ("hff", HW), ("hfb", HW), ("hi", HW),
                     ("hg", HW), ("gm", D), ("gh", D)):
        src[name] = (o, o + wd)
        o += wd
    ws, bs, col, o = [], [], {}, 0
    for name in ("mq", "mk", "mv", "mo", "hq", "hff", "hfb", "hi", "hg", "gm", "gh"):
        a, b = src[name]
        scale = M_QK_DIM ** -0.5 if name == "mq" else 1.0
        ws.append(W[:, a:b] * scale)
        bs.append(bias[a:b] * scale)
        col[name] = o
        o += b - a
    col["gates"] = o
    for g in range(M_HEADS // hps):
        for n in ("ig", "fg"):
            for off in (0, M_HEADS):
                a = src[n][0] + off + g * hps
                ws.append(W[:, a:a + hps])
                bs.append(bias[a:a + hps])
        ws.append(jnp.zeros((D, LANES - 4 * hps), F32))
        bs.append(jnp.zeros((LANES - 4 * hps,), F32))
    w_mix = jnp.concatenate(ws, axis=1).astype(BF16)
    b_mix = jnp.concatenate(bs)[None, :]

    n_gate = LANES * (M_HEADS // hps)
    lb = jnp.cumsum(jax.nn.softmax(hgrn_lb_logits.astype(F32), axis=1), axis=1)[:, 0]

    x2d = x.reshape(B * T, D)
    c2d = ctx.reshape(B * Tc, D)
    tpb = T // TOKEN_TILE
    (xm_c,) = _ffn(c2d, mc, None, row(ffn1_norm[0]), w1_in, w1_out, row(mix_norm[0]),
                   rows=(0, 1, 2), rows2=(3, 4), out_x=False, aux="mix")
    x1, xm_l = _ffn(x2d, ml, tpb, row(ffn1_norm[0]), w1_in, w1_out, row(mix_norm[0]),
                    rows=(0, 1, 2), rows2=(3, 4), out_x=True, aux="mix")

    seg = lambda name, wd, kind, arg=0: (col[name], wd, kind, arg)
    mq, mk, mv, mso, mg, sgm, sgh, hsg = _proj(
        xm_l, w_mix, b_mix, lb,
        [seg("mq", MQ, "bf16"), seg("mk", MQ, "bf16_t"), seg("mv", MV, "bf16"),
         seg("mo", MV, "sigmoid"), seg("gates", n_gate, "f32"), seg("gm", D, "sigmoid"),
         seg("gh", D, "sigmoid"), seg("hg", HW, "silu")], "proj_mlstm")
    mkc, mvc, mgc = _proj(
        xm_c, w_mix, b_mix, lb,
        [seg("mk", MQ, "bf16_t"), seg("mv", MV, "bf16"), seg("gates", n_gate, "f32")],
        "proj_mlstm_ctx")

    xm_cm = _to_col_major(xm_l.reshape(B, T, D)).reshape(B * T, D)
    hq, lff, hkf, lfb, hkb, hv = _proj(
        xm_cm, w_mix, b_mix, lb,
        [seg("hq", HW, "silu"), seg("hff", HW, "hgrn", 0), seg("hfb", HW, "hgrn", 1),
         seg("hi", HW, "bf16")], "proj_hgrn")
    lffc, hkfc, lfbc, hkbc, hvc = _proj(
        xm_c, w_mix, b_mix, lb,
        [seg("hff", HW, "hgrn", 0), seg("hfb", HW, "hgrn", 1), seg("hi", HW, "bf16")],
        "proj_hgrn_ctx")

    r3 = lambda a, t: a.reshape(B, t, a.shape[-1])
    hm = _mlstm(r3(mq, T), mk, r3(mv, T), r3(mg, T), mkc, r3(mvc, Tc), r3(mgc, Tc))
    hgrn_args = (r3(hq, T), r3(hkf, T), r3(hkb, T), r3(hv, T), r3(lff, T), r3(lfb, T),
                 r3(hkfc, Tc), r3(hkbc, Tc), r3(hvc, Tc), r3(lffc, Tc), r3(lfbc, Tc))
    block_ok = H_BLOCK * jnp.max(-jnp.log(lb)) <= H_BLOCK_MAX_EXPONENT
    hh_cm = lax.cond(block_ok, functools.partial(_hgrn, H_BLOCK), functools.partial(_hgrn, 1),
                     *hgrn_args)
    hh = _to_row_major(hh_cm)

    merge = (hm.reshape(B * T, MV), hh.reshape(B * T, HW), mso, hsg, sgm, sgh,
             row(mlstm_norm[0]), row(hgrn_norm[0]),
             proj_m[0].astype(BF16), proj_h[0].astype(BF16), mix_w_out[0].astype(BF16), 5)
    (out,) = _ffn(x1, ml, tpb, row(ffn2_norm[0]), w2_in, w2_out, row(final_norm),
                  rows=(6, 7, 8), rows2=(3, 4), out_x=False, aux="final", merge=merge)
    return out.reshape(B, T, D)
```

```python
import functools

import jax
import jax.numpy as jnp
from jax import lax
from jax.experimental import pallas as pl
from jax.experimental.pallas import tpu as pltpu

F32 = jnp.float32
BF16 = jnp.bfloat16
EPS = 1e-6

GRID_W = 64
M_HEADS = 8
M_QK_DIM = 64
M_V_DIM = 128
H_HEADS = 8
H_DIM = 128
N_MOD = 9

LANES = 128
TOKEN_TILE = 512
FFN_CHUNK = 256
PROJ_CHUNK = 512
M_CHUNK = 128
H_CHUNK = 64
H_BLOCK = 16
H_BLOCK_MAX_EXPONENT = 64.0
HEADS_PER_STEP = 4
SCAN_BODY_CHUNKS = 8
VMEM_LIMIT = 56 * 1024 * 1024

_NT = (((1,), (1,)), ((), ()))


def _params(sem, vmem=VMEM_LIMIT):
    return pltpu.CompilerParams(dimension_semantics=sem, vmem_limit_bytes=vmem)


def _resident(shape):
    nd = len(shape)
    return pl.BlockSpec(shape, lambda *_: (0,) * nd, pipeline_mode=pl.Buffered(1))


def _dot(a, b):
    return jnp.dot(a, b, preferred_element_type=F32)


def _rms(x, g):
    ms = jnp.mean(x * x, axis=-1, keepdims=True)
    return x * lax.rsqrt(ms + EPS) * g


def _split3(x):
    hi = x.astype(BF16)
    r = x - hi.astype(F32)
    mid = r.astype(BF16)
    lo = (r - mid.astype(F32)).astype(BF16)
    return hi, mid, lo


def _tri_cumsum(tri, x):
    hi, mid, _ = _split3(x)
    return _dot(tri, hi) + _dot(tri, mid)


def _mod_kernel(c_ref, w_ref, b_ref, o_ref):
    c = c_ref[...]
    a = c * jax.nn.sigmoid(c)
    a_hi = a.astype(BF16)
    a_lo = (a - a_hi.astype(F32)).astype(BF16)
    w = w_ref[...]
    w_hi = w.astype(BF16)
    w_lo = (w - w_hi.astype(F32)).astype(BF16)
    o_ref[...] = _dot(a_hi, w_hi) + _dot(a_hi, w_lo) + _dot(a_lo, w_hi) + b_ref[...]


def _modulation(cc, w, b):
    R, D = cc.shape
    N = w.shape[1]
    tn = 1024
    return pl.pallas_call(
        _mod_kernel,
        out_shape=jax.ShapeDtypeStruct((R, N), F32),
        grid=(N // tn,),
        in_specs=[pl.BlockSpec((R, D), lambda j: (0, 0)),
                  pl.BlockSpec((D, tn), lambda j: (0, j)),
                  pl.BlockSpec((1, tn), lambda j: (0, j))],
        out_specs=pl.BlockSpec((R, tn), lambda j: (0, j)),
        compiler_params=_params(("arbitrary",)),
        name="modulation",
    )(cc, w, b)


def _ffn_kernel(x_ref, mod_ref, g_ref, win_ref, wout_ref, g2_ref, *rest,
                rows, rows2, out_x, aux, n_chunks, merge_gate_row):
    fc = FFN_CHUNK
    x = x_ref[...]
    mod = mod_ref[0]
    outs = rest
    if merge_gate_row is not None:
        (hm_ref, hh_ref, om_ref, oh_ref, gm_ref, gh_ref, nm_ref, nh_ref,
         pm_ref, ph_ref, wo_ref) = rest[:11]
        outs = rest[11:]

        def branch(h_ref, gate_ref, gain_ref):
            heads = []
            for j in range(h_ref.shape[1] // LANES):
                cs = slice(j * LANES, (j + 1) * LANES)
                heads.append(_rms(h_ref[:, cs].astype(F32), gain_ref[:, cs])
                             * gate_ref[:, cs].astype(F32))
            return jnp.concatenate(heads, axis=1).astype(BF16)

        z = (gm_ref[...].astype(F32) * _dot(branch(hm_ref, om_ref, nm_ref), pm_ref[...])
             + gh_ref[...].astype(F32) * _dot(branch(hh_ref, oh_ref, nh_ref), ph_ref[...])
             ).astype(BF16)
        x = x + mod[merge_gate_row:merge_gate_row + 1] * _dot(z, wo_ref[...])
    shift = mod[rows[0]:rows[0] + 1]
    scale = mod[rows[1]:rows[1] + 1]
    gate = mod[rows[2]:rows[2] + 1]
    xm = (_rms(x, g_ref[...]) * (1.0 + scale) + shift).astype(BF16)
    acc = jnp.zeros(x.shape, F32)
    F = n_chunks * fc
    for j in range(n_chunks):
        a = _dot(xm, win_ref[:, j * fc:(j + 1) * fc])
        u = _dot(xm, win_ref[:, F + j * fc:F + (j + 1) * fc])
        h = (a * jax.nn.sigmoid(a) * u).astype(BF16)
        acc = acc + _dot(h, wout_ref[j * fc:(j + 1) * fc, :])
    xn = x + 0.5 * gate * acc
    k = 0
    if out_x:
        outs[k][...] = xn
        k += 1
    if aux == "mix":
        shift2 = mod[rows2[0]:rows2[0] + 1]
        scale2 = mod[rows2[1]:rows2[1] + 1]
        outs[k][...] = (_rms(xn, g2_ref[...]) * (1.0 + scale2) + shift2).astype(BF16)
    elif aux == "final":
        outs[k][...] = _rms(xn, g2_ref[...])


def _ffn(x2d, mod, tiles_per_mod, g, win, wout, g2, *, rows, rows2, out_x, aux, merge=None):
    N, D = x2d.shape
    F = wout.shape[0]
    tm = TOKEN_TILE
    out_shape, out_specs = [], []
    if out_x:
        out_shape.append(jax.ShapeDtypeStruct((N, D), F32))
        out_specs.append(pl.BlockSpec((tm, D), lambda i: (i, 0)))
    if aux == "mix":
        out_shape.append(jax.ShapeDtypeStruct((N, D), BF16))
        out_specs.append(pl.BlockSpec((tm, D), lambda i: (i, 0)))
    elif aux == "final":
        out_shape.append(jax.ShapeDtypeStruct((N, D), F32))
        out_specs.append(pl.BlockSpec((tm, D), lambda i: (i, 0)))
    if tiles_per_mod is None:
        mod_map = lambda i: (0, 0, 0)
    else:
        mod_map = lambda i: (i // tiles_per_mod, 0, 0)
    kern = functools.partial(_ffn_kernel, rows=rows, rows2=rows2, out_x=out_x, aux=aux,
                             n_chunks=F // FFN_CHUNK,
                             merge_gate_row=None if merge is None else merge[11])
    tile = pl.BlockSpec((tm, D), lambda i: (i, 0))
    in_specs = [tile, pl.BlockSpec((1, N_MOD, D), mod_map), _resident((1, D)),
                _resident(win.shape), _resident(wout.shape), _resident((1, D))]
    args = [x2d, mod, g, win, wout, g2]
    if merge is not None:
        in_specs += [tile] * 6 + [_resident(w.shape) for w in merge[6:11]]
        args += list(merge[:11])
    return pl.pallas_call(
        kern,
        out_shape=out_shape,
        grid=(N // tm,),
        in_specs=in_specs,
        out_specs=out_specs,
        compiler_params=_params(("parallel",)),
        name=("merge_" if merge is not None else "") + "ffn_" + aux,
    )(*args)


def _proj_kernel(x_ref, w_ref, b_ref, lb_ref, *outs, segs):
    x = x_ref[...]
    oi = 0
    for c0, wd, kind, arg in segs:
        for s in range(0, wd, PROJ_CHUNK):
            w = min(PROJ_CHUNK, wd - s)
            y = _dot(x, w_ref[:, c0 + s:c0 + s + w]) + b_ref[:, c0 + s:c0 + s + w]
            if kind == "bf16":
                outs[oi][:, s:s + w] = y.astype(BF16)
            elif kind == "bf16_t":
                outs[oi][s:s + w, :] = y.T.astype(BF16)
            elif kind == "f32":
                outs[oi][:, s:s + w] = y
            elif kind == "sigmoid":
                outs[oi][:, s:s + w] = jax.nn.sigmoid(y).astype(BF16)
            elif kind == "silu":
                outs[oi][:, s:s + w] = (y * jax.nn.sigmoid(y)).astype(BF16)
            elif kind == "hgrn":
                lb = lb_ref[arg:arg + 1, s:s + w]
                sg = jax.nn.sigmoid(y)
                outs[oi][:, s:s + w] = jnp.log(lb + (1.0 - lb) * sg)
                outs[oi + 1][:, s:s + w] = ((1.0 - lb) * (1.0 - sg)).astype(BF16)
        oi += 2 if kind == "hgrn" else 1


def _proj(x2d, w, b, lb, segs, name):
    N, D = x2d.shape
    tm = TOKEN_TILE
    out_shape, out_specs = [], []
    for _, wd, kind, _ in segs:
        if kind == "bf16_t":
            out_shape.append(jax.ShapeDtypeStruct((wd, N), BF16))
            out_specs.append(pl.BlockSpec((wd, tm), lambda i: (0, i)))
            continue
        dts = {"bf16": [BF16], "f32": [F32], "sigmoid": [BF16], "silu": [BF16],
               "hgrn": [F32, BF16]}[kind]
        for dt in dts:
            out_shape.append(jax.ShapeDtypeStruct((N, wd), dt))
            out_specs.append(pl.BlockSpec((tm, wd), lambda i: (i, 0)))
    return pl.pallas_call(
        functools.partial(_proj_kernel, segs=tuple(segs)),
        out_shape=out_shape,
        grid=(N // tm,),
        in_specs=[pl.BlockSpec((tm, D), lambda i: (i, 0)),
                  _resident(w.shape),
                  _resident(b.shape),
                  _resident(lb.shape)],
        out_specs=out_specs,
        compiler_params=_params(("parallel",)),
        name=name,
    )(x2d, w, b, lb)


def _mlstm_kernel(q_ref, kt_ref, v_ref, g_ref, ktc_ref, vc_ref, gc_ref,
                  out_ref, hacc_ref, c_ref, mprev_ref, rows_ref,
                  s0_ref, wi0_ref, fl0_ref, s1_ref, wi1_ref, fl1_ref):
    L = M_CHUNK
    hps = HEADS_PER_STEP
    T = q_ref.shape[0]
    Tc = vc_ref.shape[0]
    nt, nct = T // L, Tc // L
    assert nt % SCAN_BODY_CHUNKS == 0 and SCAN_BODY_CHUNKS % 2 == 0
    row = lax.broadcasted_iota(jnp.int32, (L, L), 0)
    col = lax.broadcasted_iota(jnp.int32, (L, L), 1)
    lane = lax.broadcasted_iota(jnp.int32, (L, LANES), 1)
    krow = lax.broadcasted_iota(jnp.int32, (LANES, 1), 0)
    ones_ext = jnp.ones((L, LANES), BF16)
    R_R, R_LF, R_W, R_M, R_DEC = (k * hps for k in range(5))

    for d in (0, 1):
        tri = (col <= row) if d == 0 else (col >= row)
        trif_t = ((col >= row) if d == 0 else (col <= row)).astype(F32)
        c_ref[...] = jnp.zeros(c_ref.shape, F32)

        def gate_body(n, carry, gref, slot0, d=d, trif_t=trif_t):
            GT = gref[pl.ds(pl.multiple_of(n * L, L), L), :].T[:4 * hps, :]
            lf = jax.nn.log_sigmoid(GT[(2 + d) * hps:(3 + d) * hps, :])
            b = sum(_dot(part.astype(F32), trif_t) for part in _split3(lf))
            r = GT[d * hps:(d + 1) * hps, :] - b
            b_last = b[:, L - 1:L] if d == 0 else b[:, 0:1]
            wlog = b_last + r
            wide = lambda x: jnp.broadcast_to(x, (hps, L))
            rows_ref[slot0 + n] = jnp.concatenate(
                [r, lf, wlog, wide(b_last), wide(jnp.max(wlog, axis=1, keepdims=True)),
                 jnp.zeros((rows_ref.shape[1] - 5 * hps, L), F32)], axis=0)
            return carry

        def stab_body(n, m_prev, nchunks, slot0, d=d):
            idx = slot0 + (n if d == 0 else nchunks - 1 - n)
            mprev_ref[idx] = m_prev
            return jnp.maximum(rows_ref[idx, R_M:R_M + hps, :] + m_prev,
                               rows_ref[idx, R_DEC:R_DEC + hps, :])

        def weight_body(n, carry):
            R = rows_ref[n]
            m_prev = mprev_ref[n]
            bm = R[R_M:R_M + hps, :] + m_prev
            m_new = jnp.maximum(bm, R[R_DEC:R_DEC + hps, :])
            rows_ref[n, R_W:R_W + 3 * hps, :] = jnp.concatenate(
                [jnp.exp(R[R_W:R_W + hps, :] - m_new), m_prev, jnp.exp(bm - m_new)], axis=0)
            return carry

        lax.fori_loop(0, nct, functools.partial(gate_body, gref=gc_ref, slot0=0), 0, unroll=True)
        lax.fori_loop(0, nt, functools.partial(gate_body, gref=g_ref, slot0=nct), 0, unroll=8)
        m_run = lax.fori_loop(0, nct, functools.partial(stab_body, nchunks=nct, slot0=0),
                              jnp.zeros((hps, L), F32), unroll=True)
        lax.fori_loop(0, nt, functools.partial(stab_body, nchunks=nt, slot0=nct), m_run, unroll=True)
        lax.fori_loop(0, nct + nt, weight_body, 0, unroll=6)

        def update_state(p, R, rows, ktref, vref):
            kT = ktref[p * LANES:(p + 1) * LANES, rows].astype(F32)
            upd = None
            decay_rows = None
            for a in range(2):
                i = 2 * p + a
                vext = jnp.concatenate([vref[rows, i * LANES:(i + 1) * LANES], ones_ext], axis=1)
                rowmask = (krow < M_QK_DIM) if a == 0 else (krow >= M_QK_DIM)
                kw = jnp.where(rowmask, kT * R[R_W + i:R_W + i + 1, :], 0.0)
                u = _dot(kw.astype(BF16).astype(F32), vext.astype(F32))
                upd = u if upd is None else upd + u
                dr = jnp.where(rowmask, R[R_DEC + i:R_DEC + i + 1, 0:1], 0.0)
                decay_rows = dr if decay_rows is None else decay_rows + dr
            c_ref[p] = decay_rows * c_ref[p] + upd

        def ctx_body(n, carry, d=d, update_state=update_state):
            c = n if d == 0 else nct - 1 - n
            rows = pl.ds(pl.multiple_of(c * L, L), L)
            for p in range(hps // 2):
                update_state(p, rows_ref[c], rows, ktc_ref, vc_ref)
            return carry

        lax.fori_loop(0, nct, ctx_body, 0, unroll=True)

        def head_q(rows, p, a):
            q2 = q_ref[rows, p * LANES:(p + 1) * LANES]
            headmask = (lane < M_QK_DIM) if a == 0 else (lane >= M_QK_DIM)
            return jnp.where(headmask, q2, jnp.zeros_like(q2))

        def pair_weights(c, s_ref, wi_ref, fl_ref, tri=tri, head_q=head_q):
            rows = pl.ds(pl.multiple_of(c * L, L), L)
            R = rows_ref[nct + c]
            for i in range(hps):
                p, a = divmod(i, 2)
                s_raw = _dot(head_q(rows, p, a), kt_ref[p * LANES:(p + 1) * LANES, rows])
                m_prev = R[R_M + i:R_M + i + 1, 0:1]
                rm = jnp.where(tri, R[R_R + i:R_R + i + 1, :], -jnp.inf)
                mu = jnp.maximum(m_prev, jnp.max(rm, axis=1, keepdims=True))
                bcol = jnp.sum(jnp.where(tri, R[R_LF + i:R_LF + i + 1, :], 0.0), axis=1, keepdims=True)
                mu_b = jnp.broadcast_to(mu, (L, L))
                s_ref[i] = s_raw * jnp.exp(rm - mu_b)
                wi_ref[i] = jnp.exp(m_prev - mu_b)
                fl_ref[i] = jnp.exp(-jnp.broadcast_to(bcol + mu, (L, LANES)))

        def emit(c, s_ref, wi_ref, fl_ref, d=d, head_q=head_q, update_state=update_state):
            rows = pl.ds(pl.multiple_of(c * L, L), L)
            for p in range(hps // 2):
                C2b = c_ref[p].astype(BF16)
                for a in range(2):
                    i = 2 * p + a
                    cs = slice(i * LANES, (i + 1) * LANES)
                    vext = jnp.concatenate([v_ref[rows, cs], ones_ext], axis=1)
                    nd = (jnp.concatenate([wi_ref[i]] * 2, axis=1) * _dot(head_q(rows, p, a), C2b)
                          + _dot(s_ref[i].astype(BF16), vext))
                    h = nd[:, :LANES] / jnp.maximum(jnp.abs(nd[:, LANES:]), fl_ref[i])
                    if d == 0:
                        hacc_ref[rows, cs] = h
                    else:
                        out_ref[rows, cs] = (hacc_ref[rows, cs] + h).astype(BF16)
                update_state(p, rows_ref[nct + c], rows, kt_ref, v_ref)

        order = (lambda n: n) if d == 0 else (lambda n: nt - 1 - n)
        slot0 = (s0_ref, wi0_ref, fl0_ref)
        slot1 = (s1_ref, wi1_ref, fl1_ref)

        def lat_body(n, carry, pair_weights=pair_weights, emit=emit, order=order):
            slots = (slot0, slot1)
            for k in range(SCAN_BODY_CHUNKS):
                c = SCAN_BODY_CHUNKS * n + k
                pair_weights(order(jnp.minimum(c + 1, nt - 1)), *slots[(k + 1) % 2])
                emit(order(c), *slots[k % 2])
            return carry

        pair_weights(order(0), *slot0)
        lax.fori_loop(0, nt // SCAN_BODY_CHUNKS, lat_body, 0)


def _mlstm(q, kt, v, g, ktc, vc, gc):
    B, T, _ = q.shape
    Tc = vc.shape[1]
    hps = HEADS_PER_STEP
    G = M_HEADS // hps
    wq, wv = hps * M_QK_DIM, hps * M_V_DIM
    lat = lambda w: pl.BlockSpec((None, T, w), lambda b, g: (b, 0, g))
    cx = lambda w: pl.BlockSpec((None, Tc, w), lambda b, g: (b, 0, g))
    return pl.pallas_call(
        _mlstm_kernel,
        out_shape=jax.ShapeDtypeStruct((B, T, M_HEADS * M_V_DIM), BF16),
        grid=(B, G),
        in_specs=[lat(wq), pl.BlockSpec((wq, T), lambda b, g: (g, b)), lat(wv), lat(LANES),
                  pl.BlockSpec((wq, Tc), lambda b, g: (g, b)), cx(wv), cx(LANES)],
        out_specs=lat(wv),
        scratch_shapes=[pltpu.VMEM((T, wv), F32),
                        pltpu.VMEM((hps // 2, 2 * M_QK_DIM, 2 * LANES), F32),
                        pltpu.VMEM(((T + Tc) // M_CHUNK, hps, M_CHUNK), F32),
                        pltpu.VMEM(((T + Tc) // M_CHUNK, 8 * hps, M_CHUNK), F32)]
                       + [pltpu.VMEM((hps, M_CHUNK, M_CHUNK), F32)] * 6,
        compiler_params=_params(("parallel", "parallel")),
        name="mlstm_scan",
    )(q, kt, v, g, ktc, vc, gc)


def _pivot_rows(b, h, off):
    L, N = b.shape
    bc = lambda r, n: jnp.broadcast_to(b[r:r + 1, :], (n, N))
    if 2 * h >= 8:
        return jnp.concatenate([bc(x + off, 2 * h) for x in range(0, L, 2 * h)], axis=0)
    sub = lax.broadcasted_iota(jnp.int32, (8, N), 0)
    tiles = []
    for x in range(0, L, 8):
        t = bc(x + 8 - 2 * h + off, 8)
        for y in range(8 - 4 * h, -1, -2 * h):
            t = jnp.where(sub < y + 2 * h, bc(x + y + off, 8), t)
        tiles.append(t)
    return jnp.concatenate(tiles, axis=0)


def _block_entry_rows(b, beta, d):
    L, N = b.shape
    tiles = []
    for x in range(0, L, beta):
        r = x - 1 if d == 0 else x + beta
        inside = 0 <= r < L
        tiles.append(jnp.broadcast_to(b[r:r + 1, :], (beta, N)) if inside else jnp.zeros((beta, N), F32))
    return jnp.concatenate(tiles, axis=0)


def _hgrn_kernel(q_ref, kf_ref, kb_ref, v_ref, lff_ref, lfb_ref,
                 kfc_ref, kbc_ref, vc_ref, lffc_ref, lfbc_ref,
                 out_ref, hacc_ref, st_ref, b_ref, bc_ref, attn0_ref, attn1_ref, *, beta):
    L = H_CHUNK
    hps = HEADS_PER_STEP
    T = q_ref.shape[0]
    Tc = vc_ref.shape[0]
    nt, nct = T // L, Tc // L
    assert nt % SCAN_BODY_CHUNKS == 0 and SCAN_BODY_CHUNKS % 2 == 0
    row = lax.broadcasted_iota(jnp.int32, (L, L), 0)
    col = lax.broadcasted_iota(jnp.int32, (L, L), 1)
    halves = [h for h in (L >> (n + 1) for n in range(L.bit_length() - 1)) if h >= beta]
    zpad_f = jnp.zeros((H_DIM - L, H_DIM), F32)
    zpad_b = jnp.zeros((H_DIM - L, H_DIM), BF16)
    neg_log2e = -1.4426950408889634

    for d in (0, 1):
        tri = (col <= row) if d == 0 else (col >= row)
        trib = tri.astype(BF16)
        last = L - 1 if d == 0 else 0
        st_ref[...] = jnp.zeros(st_ref.shape, F32)
        k_lat, lf_lat = (kf_ref, lff_ref) if d == 0 else (kb_ref, lfb_ref)
        k_ctx, lf_ctx = (kfc_ref, lffc_ref) if d == 0 else (kbc_ref, lfbc_ref)
        local = tri & ((row & -beta) == (col & -beta))
        pair = []
        for h in halves:
            same = (row & -(2 * h)) == (col & -(2 * h))
            r_late = (row & (2 * h - 1)) >= h
            c_late = (col & (2 * h - 1)) >= h
            pair.append(same & (r_late & ~c_late if d == 0 else ~r_late & c_late))

        def cumsum_body(n, carry, lfref, bref, trib=trib):
            rows = pl.ds(pl.multiple_of(n * L, L), L)
            bref[rows, :] = _tri_cumsum(trib, lfref[rows, :])
            return carry

        lax.fori_loop(0, nct, functools.partial(cumsum_body, lfref=lf_ctx, bref=bc_ref), 0, unroll=True)
        lax.fori_loop(0, nt, functools.partial(cumsum_body, lfref=lf_lat, bref=b_ref), 0, unroll=8)

        def update_state(i, rows, cs, kref, vref, bref):
            b_h = bref[rows, cs]
            bl = b_h[last:last + 1, :]
            kd = (kref[rows, cs].astype(F32) * jnp.exp(bl - b_h)).astype(BF16)
            vT = jnp.concatenate([vref[rows, cs].astype(F32), zpad_f], axis=0).T.astype(BF16)
            st_ref[i] = (st_ref[i] * jnp.exp(bl)
                         + _dot(vT, jnp.concatenate([kd, zpad_b], axis=0)))

        def ctx_body(n, carry, d=d, k_ctx=k_ctx, update_state=update_state):
            c = n if d == 0 else nct - 1 - n
            rows = pl.ds(pl.multiple_of(c * L, L), L)
            for i in range(hps):
                update_state(i, rows, slice(i * H_DIM, (i + 1) * H_DIM), k_ctx, vc_ref, bc_ref)
            return carry

        lax.fori_loop(0, nct, ctx_body, 0, unroll=True)

        def pair_weights(c, attn_ref, d=d, k_lat=k_lat, local=local, pair=pair):
            rows = pl.ds(pl.multiple_of(c * L, L), L)
            for i in range(hps):
                cs = slice(i * H_DIM, (i + 1) * H_DIM)
                b_h = b_ref[rows, cs]
                q_h = q_ref[rows, cs].astype(F32)
                k_h = k_lat[rows, cs].astype(F32)
                if beta > 1:
                    dl = (b_h - _block_entry_rows(b_h, beta, d)) * neg_log2e
                    qs = (q_h * jnp.exp2(-dl)).astype(BF16)
                    ks = (k_h * jnp.exp2(dl)).astype(BF16)
                else:
                    qs, ks = q_ref[rows, cs], k_lat[rows, cs]
                attn = jnp.where(local, lax.dot_general(qs, ks, _NT, preferred_element_type=F32), 0.0)
                for n, h in enumerate(halves):
                    z = jnp.exp2(jnp.abs(b_h - _pivot_rows(b_h, h, h - 1 + d)) * neg_log2e)
                    qk = lax.dot_general((q_h * z).astype(BF16), (k_h * z).astype(BF16), _NT,
                                         preferred_element_type=F32)
                    attn = jnp.where(pair[n], qk, attn)
                attn_ref[i] = attn

        def emit(c, attn_ref, d=d, k_lat=k_lat, update_state=update_state):
            rows = pl.ds(pl.multiple_of(c * L, L), L)
            for i in range(hps):
                cs = slice(i * H_DIM, (i + 1) * H_DIM)
                qc = (q_ref[rows, cs].astype(F32) * jnp.exp(b_ref[rows, cs])).astype(BF16)
                o = (lax.dot_general(qc, st_ref[i].astype(BF16), _NT, preferred_element_type=F32)
                     + _dot(attn_ref[i].astype(BF16), v_ref[rows, cs]))
                if d == 0:
                    hacc_ref[rows, cs] = o
                else:
                    out_ref[rows, cs] = (hacc_ref[rows, cs] + o).astype(BF16)
                update_state(i, rows, cs, k_lat, v_ref, b_ref)

        order = (lambda n: n) if d == 0 else (lambda n: nt - 1 - n)

        def lat_body(n, carry, pair_weights=pair_weights, emit=emit, order=order):
            slots = (attn0_ref, attn1_ref)
            for k in range(SCAN_BODY_CHUNKS):
                c = SCAN_BODY_CHUNKS * n + k
                pair_weights(order(jnp.minimum(c + 1, nt - 1)), slots[(k + 1) % 2])
                emit(order(c), slots[k % 2])
            return carry

        pair_weights(order(0), attn0_ref)
        lax.fori_loop(0, nt // SCAN_BODY_CHUNKS, lat_body, 0)


def _hgrn(beta, q, kf, kb, v, lff, lfb, kfc, kbc, vc, lffc, lfbc):
    B, T, W = q.shape
    Tc = vc.shape[1]
    hps = HEADS_PER_STEP
    G = H_HEADS // hps
    w = hps * H_DIM
    lat = pl.BlockSpec((None, T, w), lambda b, g: (b, 0, g))
    cx = pl.BlockSpec((None, Tc, w), lambda b, g: (b, 0, g))
    return pl.pallas_call(
        functools.partial(_hgrn_kernel, beta=beta),
        out_shape=jax.ShapeDtypeStruct((B, T, W), BF16),
        grid=(B, G),
        in_specs=[lat] * 6 + [cx] * 5,
        out_specs=lat,
        scratch_shapes=[pltpu.VMEM((T, w), F32),
                        pltpu.VMEM((hps, H_DIM, H_DIM), F32),
                        pltpu.VMEM((T, w), F32),
                        pltpu.VMEM((Tc, w), F32),
                        pltpu.VMEM((hps, H_CHUNK, H_CHUNK), F32),
                        pltpu.VMEM((hps, H_CHUNK, H_CHUNK), F32)],
        compiler_params=_params(("parallel", "parallel")),
        name="hgrn_scan_b%d" % beta,
    )(q, kf, kb, v, lff, lfb, kfc, kbc, vc, lffc, lfbc)


def _to_col_major(a):
    B, T, C = a.shape
    return a.reshape(B, T // GRID_W, GRID_W, C).transpose(0, 2, 1, 3).reshape(B, T, C)


def _to_row_major(a):
    B, T, C = a.shape
    return a.reshape(B, GRID_W, T // GRID_W, C).transpose(0, 2, 1, 3).reshape(B, T, C)


def kernel(x, c, ctx, c_ctx, ada_w, ada_b, ffn1_norm, ffn1_w_in, ffn1_w_out, mix_norm,
           mix_w_in, mix_b_in, mlstm_norm, hgrn_lb_logits, hgrn_norm, proj_m, proj_h,
           mix_w_out, ffn2_norm, ffn2_w_in, ffn2_w_out, final_norm):
    B, T, D = x.shape
    Tc = ctx.shape[1]
    hps = HEADS_PER_STEP
    MQ, MV, HW = M_HEADS * M_QK_DIM, M_HEADS * M_V_DIM, H_HEADS * H_DIM

    pad = (-(B + 1)) % 8
    cc = jnp.concatenate([c, c_ctx[None, :], jnp.zeros((pad, D), F32)], axis=0)
    mod = _modulation(cc, ada_w[0], ada_b[0][None, :])
    ml = mod[:B].reshape(B, N_MOD, D)
    mc = mod[B:B + 1].reshape(1, N_MOD, D)

    row = lambda v: v.reshape(1, -1).astype(F32)
    w1_in = ffn1_w_in[0].astype(BF16)
    w1_out = ffn1_w_out[0].astype(BF16)
    w2_in = ffn2_w_in[0].astype(BF16)
    w2_out = ffn2_w_out[0].astype(BF16)

    W, bias = mix_w_in[0], mix_b_in[0]
    o = 0
    src = {}
    for name, wd in (("mq", MQ), ("mk", MQ), ("mv", MV), ("mo", MV), ("ig", 2 * M_HEADS),
                     ("fg", 2 * M_HEADS), ("hq", HW), ("hff", HW), ("hfb", HW), ("hi", HW),
                     ("hg", HW), ("gm", D), ("gh", D)):
        src[name] = (o, o + wd)
        o += wd
    ws, bs, col, o = [], [], {}, 0
    for name in ("mq", "mk", "mv", "mo", "hq", "hff", "hfb", "hi", "hg", "gm", "gh"):
        a, b = src[name]
        scale = M_QK_DIM ** -0.5 if name == "mq" else 1.0
        ws.append(W[:, a:b] * scale)
        bs.append(bias[a:b] * scale)
        col[name] = o
        o += b - a
    col["gates"] = o
    for g in range(M_HEADS // hps):
        for n in ("ig", "fg"):
            for off in (0, M_HEADS):
                a = src[n][0] + off + g * hps
                ws.append(W[:, a:a + hps])
                bs.append(bias[a:a + hps])
        ws.append(jnp.zeros((D, LANES - 4 * hps), F32))
        bs.append(jnp.zeros((LANES - 4 * hps,), F32))
    w_mix = jnp.concatenate(ws, axis=1).astype(BF16)
    b_mix = jnp.concatenate(bs)[None, :]

    n_gate = LANES * (M_HEADS // hps)
    lb = jnp.cumsum(jax.nn.softmax(hgrn_lb_logits.astype(F32), axis=1), axis=1)[:, 0]

    x2d = x.reshape(B * T, D)
    c2d = ctx.reshape(B * Tc, D)
    tpb = T // TOKEN_TILE
    (xm_c,) = _ffn(c2d, mc, None, row(ffn1_norm[0]), w1_in, w1_out, row(mix_norm[0]),
                   rows=(0, 1, 2), rows2=(3, 4), out_x=False, aux="mix")
    x1, xm_l = _ffn(x2d, ml, tpb, row(ffn1_norm[0]), w1_in, w1_out, row(mix_norm[0]),
                    rows=(0, 1, 2), rows2=(3, 4), out_x=True, aux="mix")

    seg = lambda name, wd, kind, arg=0: (col[name], wd, kind, arg)
    mq, mk, mv, mso, mg, sgm, sgh, hsg = _proj(
        xm_l, w_mix, b_mix, lb,
        [seg("mq", MQ, "bf16"), seg("mk", MQ, "bf16_t"), seg("mv", MV, "bf16"),
         seg("mo", MV, "sigmoid"), seg("gates", n_gate, "f32"), seg("gm", D, "sigmoid"),
         seg("gh", D, "sigmoid"), seg("hg", HW, "silu")], "proj_mlstm")
    mkc, mvc, mgc = _proj(
        xm_c, w_mix, b_mix, lb,
        [seg("mk", MQ, "bf16_t"), seg("mv", MV, "bf16"), seg("gates", n_gate, "f32")],
        "proj_mlstm_ctx")

    xm_cm = _to_col_major(xm_l.reshape(B, T, D)).reshape(B * T, D)
    hq, lff, hkf, lfb, hkb, hv = _proj(
        xm_cm, w_mix, b_mix, lb,
        [seg("hq", HW, "silu"), seg("hff", HW, "hgrn", 0), seg("hfb", HW, "hgrn", 1),
         seg("hi", HW, "bf16")], "proj_hgrn")
    lffc, hkfc, lfbc, hkbc, hvc = _proj(
        xm_c, w_mix, b_mix, lb,
        [seg("hff", HW, "hgrn", 0), seg("hfb", HW, "hgrn", 1), seg("hi", HW, "bf16")],
        "proj_hgrn_ctx")

    r3 = lambda a, t: a.reshape(B, t, a.shape[-1])
    hm = _mlstm(r3(mq, T), mk, r3(mv, T), r3(mg, T), mkc, r3(mvc, Tc), r3(mgc, Tc))
    hgrn_args = (r3(hq, T), r3(hkf, T), r3(hkb, T), r3(hv, T), r3(lff, T), r3(lfb, T),
                 r3(hkfc, Tc), r3(hkbc, Tc), r3(hvc, Tc), r3(lffc, Tc), r3(lfbc, Tc))
    block_ok = H_BLOCK * jnp.max(-jnp.log(lb)) <= H_BLOCK_MAX_EXPONENT
    hh_cm = lax.cond(block_ok, functools.partial(_hgrn, H_BLOCK), functools.partial(_hgrn, 1),
                     *hgrn_args)
    hh = _to_row_major(hh_cm)

    merge = (hm.reshape(B * T, MV), hh.reshape(B * T, HW), mso, hsg, sgm, sgh,
             row(mlstm_norm[0]), row(hgrn_norm[0]),
             proj_m[0].astype(BF16), proj_h[0].astype(BF16), mix_w_out[0].astype(BF16), 5)
    (out,) = _ffn(x1, ml, tpb, row(ffn2_norm[0]), w2_in, w2_out, row(final_norm),
                  rows=(6, 7, 8), rows2=(3, 4), out_x=False, aux="final", merge=merge)
    return out.reshape(B, T, D)
```

```python
import functools

import jax
import jax.numpy as jnp
from jax import lax
from jax.experimental import pallas as pl
from jax.experimental.pallas import tpu as pltpu

F32 = jnp.float32
BF16 = jnp.bfloat16
EPS = 1e-6

GRID_W = 64
M_HEADS = 8
M_QK_DIM = 64
M_V_DIM = 128
H_HEADS = 8
H_DIM = 128
N_MOD = 9

LANES = 128
TOKEN_TILE = 512
FFN_CHUNK = 256
PROJ_CHUNK = 512
M_CHUNK = 128
H_CHUNK = 64
H_BLOCK = 16
H_BLOCK_MAX_EXPONENT = 64.0
HEADS_PER_STEP = 4
M_BODY_CHUNKS = 8
H_BODY_CHUNKS = 16
VMEM_LIMIT = 56 * 1024 * 1024

_NT = (((1,), (1,)), ((), ()))


def _params(sem, vmem=VMEM_LIMIT):
    return pltpu.CompilerParams(dimension_semantics=sem, vmem_limit_bytes=vmem)


def _resident(shape):
    nd = len(shape)
    return pl.BlockSpec(shape, lambda *_: (0,) * nd, pipeline_mode=pl.Buffered(1))


def _dot(a, b):
    return jnp.dot(a, b, preferred_element_type=F32)


def _rms(x, g):
    ms = jnp.mean(x * x, axis=-1, keepdims=True)
    return x * lax.rsqrt(ms + EPS) * g


def _split3(x):
    hi = x.astype(BF16)
    r = x - hi.astype(F32)
    mid = r.astype(BF16)
    lo = (r - mid.astype(F32)).astype(BF16)
    return hi, mid, lo


def _tri_cumsum(tri, x):
    hi, mid, _ = _split3(x)
    return _dot(tri, hi) + _dot(tri, mid)


def _mod_kernel(c_ref, w_ref, b_ref, o_ref):
    c = c_ref[...]
    a = c * jax.nn.sigmoid(c)
    a_hi = a.astype(BF16)
    a_lo = (a - a_hi.astype(F32)).astype(BF16)
    w = w_ref[...]
    w_hi = w.astype(BF16)
    w_lo = (w - w_hi.astype(F32)).astype(BF16)
    o_ref[...] = _dot(a_hi, w_hi) + _dot(a_hi, w_lo) + _dot(a_lo, w_hi) + b_ref[...]


def _modulation(cc, w, b):
    R, D = cc.shape
    N = w.shape[1]
    tn = 1024
    return pl.pallas_call(
        _mod_kernel,
        out_shape=jax.ShapeDtypeStruct((R, N), F32),
        grid=(N // tn,),
        in_specs=[pl.BlockSpec((R, D), lambda j: (0, 0)),
                  pl.BlockSpec((D, tn), lambda j: (0, j)),
                  pl.BlockSpec((1, tn), lambda j: (0, j))],
        out_specs=pl.BlockSpec((R, tn), lambda j: (0, j)),
        compiler_params=_params(("arbitrary",)),
        name="modulation",
    )(cc, w, b)


def _ffn_kernel(x_ref, mod_ref, g_ref, win_ref, wout_ref, g2_ref, *rest,
                rows, rows2, out_x, aux, n_chunks, merge_gate_row):
    fc = FFN_CHUNK
    x = x_ref[...]
    mod = mod_ref[0]
    outs = rest
    if merge_gate_row is not None:
        (hm_ref, hh_ref, om_ref, oh_ref, gm_ref, gh_ref, nm_ref, nh_ref,
         pm_ref, ph_ref, wo_ref) = rest[:11]
        outs = rest[11:]

        def branch(h_ref, gate_ref, gain_ref):
            heads = []
            for j in range(h_ref.shape[1] // LANES):
                cs = slice(j * LANES, (j + 1) * LANES)
                heads.append(_rms(h_ref[:, cs].astype(F32), gain_ref[:, cs])
                             * gate_ref[:, cs].astype(F32))
            return jnp.concatenate(heads, axis=1).astype(BF16)

        z = (gm_ref[...].astype(F32) * _dot(branch(hm_ref, om_ref, nm_ref), pm_ref[...])
             + gh_ref[...].astype(F32) * _dot(branch(hh_ref, oh_ref, nh_ref), ph_ref[...])
             ).astype(BF16)
        x = x + mod[merge_gate_row:merge_gate_row + 1] * _dot(z, wo_ref[...])
    shift = mod[rows[0]:rows[0] + 1]
    scale = mod[rows[1]:rows[1] + 1]
    gate = mod[rows[2]:rows[2] + 1]
    xm = (_rms(x, g_ref[...]) * (1.0 + scale) + shift).astype(BF16)
    acc = jnp.zeros(x.shape, F32)
    F = n_chunks * fc
    for j in range(n_chunks):
        a = _dot(xm, win_ref[:, j * fc:(j + 1) * fc])
        u = _dot(xm, win_ref[:, F + j * fc:F + (j + 1) * fc])
        h = (a * jax.nn.sigmoid(a) * u).astype(BF16)
        acc = acc + _dot(h, wout_ref[j * fc:(j + 1) * fc, :])
    xn = x + 0.5 * gate * acc
    k = 0
    if out_x:
        outs[k][...] = xn
        k += 1
    if aux == "mix":
        shift2 = mod[rows2[0]:rows2[0] + 1]
        scale2 = mod[rows2[1]:rows2[1] + 1]
        outs[k][...] = (_rms(xn, g2_ref[...]) * (1.0 + scale2) + shift2).astype(BF16)
    elif aux == "final":
        outs[k][...] = _rms(xn, g2_ref[...])


def _ffn(x2d, mod, tiles_per_mod, g, win, wout, g2, *, rows, rows2, out_x, aux, merge=None):
    N, D = x2d.shape
    F = wout.shape[0]
    tm = TOKEN_TILE
    out_shape, out_specs = [], []
    if out_x:
        out_shape.append(jax.ShapeDtypeStruct((N, D), F32))
        out_specs.append(pl.BlockSpec((tm, D), lambda i: (i, 0)))
    if aux == "mix":
        out_shape.append(jax.ShapeDtypeStruct((N, D), BF16))
        out_specs.append(pl.BlockSpec((tm, D), lambda i: (i, 0)))
    elif aux == "final":
        out_shape.append(jax.ShapeDtypeStruct((N, D), F32))
        out_specs.append(pl.BlockSpec((tm, D), lambda i: (i, 0)))
    if tiles_per_mod is None:
        mod_map = lambda i: (0, 0, 0)
    else:
        mod_map = lambda i: (i // tiles_per_mod, 0, 0)
    kern = functools.partial(_ffn_kernel, rows=rows, rows2=rows2, out_x=out_x, aux=aux,
                             n_chunks=F // FFN_CHUNK,
                             merge_gate_row=None if merge is None else merge[11])
    tile = pl.BlockSpec((tm, D), lambda i: (i, 0))
    in_specs = [tile, pl.BlockSpec((1, N_MOD, D), mod_map), _resident((1, D)),
                _resident(win.shape), _resident(wout.shape), _resident((1, D))]
    args = [x2d, mod, g, win, wout, g2]
    if merge is not None:
        in_specs += [tile] * 6 + [_resident(w.shape) for w in merge[6:11]]
        args += list(merge[:11])
    return pl.pallas_call(
        kern,
        out_shape=out_shape,
        grid=(N // tm,),
        in_specs=in_specs,
        out_specs=out_specs,
        compiler_params=_params(("parallel",)),
        name=("merge_" if merge is not None else "") + "ffn_" + aux,
    )(*args)


def _proj_kernel(x_ref, w_ref, b_ref, lb_ref, *outs, segs):
    x = x_ref[...]
    oi = 0
    for c0, wd, kind, arg in segs:
        for s in range(0, wd, PROJ_CHUNK):
            w = min(PROJ_CHUNK, wd - s)
            y = _dot(x, w_ref[:, c0 + s:c0 + s + w]) + b_ref[:, c0 + s:c0 + s + w]
            if kind == "bf16":
                outs[oi][:, s:s + w] = y.astype(BF16)
            elif kind == "bf16_t":
                outs[oi][s:s + w, :] = y.T.astype(BF16)
            elif kind == "f32":
                outs[oi][:, s:s + w] = y
            elif kind == "sigmoid":
                outs[oi][:, s:s + w] = jax.nn.sigmoid(y).astype(BF16)
            elif kind == "silu":
                outs[oi][:, s:s + w] = (y * jax.nn.sigmoid(y)).astype(BF16)
            elif kind == "hgrn":
                lb = lb_ref[arg:arg + 1, s:s + w]
                sg = jax.nn.sigmoid(y)
                outs[oi][:, s:s + w] = jnp.log(lb + (1.0 - lb) * sg)
                outs[oi + 1][:, s:s + w] = ((1.0 - lb) * (1.0 - sg)).astype(BF16)
        oi += 2 if kind == "hgrn" else 1


def _proj(x2d, w, b, lb, segs, name):
    N, D = x2d.shape
    tm = TOKEN_TILE
    out_shape, out_specs = [], []
    for _, wd, kind, _ in segs:
        if kind == "bf16_t":
            out_shape.append(jax.ShapeDtypeStruct((wd, N), BF16))
            out_specs.append(pl.BlockSpec((wd, tm), lambda i: (0, i)))
            continue
        dts = {"bf16": [BF16], "f32": [F32], "sigmoid": [BF16], "silu": [BF16],
               "hgrn": [F32, BF16]}[kind]
        for dt in dts:
            out_shape.append(jax.ShapeDtypeStruct((N, wd), dt))
            out_specs.append(pl.BlockSpec((tm, wd), lambda i: (i, 0)))
    return pl.pallas_call(
        functools.partial(_proj_kernel, segs=tuple(segs)),
        out_shape=out_shape,
        grid=(N // tm,),
        in_specs=[pl.BlockSpec((tm, D), lambda i: (i, 0)),
                  _resident(w.shape),
                  _resident(b.shape),
                  _resident(lb.shape)],
        out_specs=out_specs,
        compiler_params=_params(("parallel",)),
        name=name,
    )(x2d, w, b, lb)


def _mlstm_kernel(q_ref, kt_ref, v_ref, g_ref, ktc_ref, vc_ref, gc_ref,
                  out_ref, hacc_ref, c_ref, mprev_ref, rows_ref,
                  s0_ref, wi0_ref, fl0_ref, s1_ref, wi1_ref, fl1_ref):
    L = M_CHUNK
    hps = HEADS_PER_STEP
    T = q_ref.shape[0]
    Tc = vc_ref.shape[0]
    nt, nct = T // L, Tc // L
    body_chunks = min(M_BODY_CHUNKS, nt)
    assert nt % body_chunks == 0 and body_chunks % 2 == 0
    row = lax.broadcasted_iota(jnp.int32, (L, L), 0)
    col = lax.broadcasted_iota(jnp.int32, (L, L), 1)
    lane = lax.broadcasted_iota(jnp.int32, (L, LANES), 1)
    krow = lax.broadcasted_iota(jnp.int32, (LANES, 1), 0)
    ones_ext = jnp.ones((L, LANES), BF16)
    R_R, R_LF, R_W, R_M, R_DEC = (k * hps for k in range(5))

    nc = nct + nt
    tris = [(col <= row), (col >= row)]

    def gate_body(n, carry, gref, slot0):
        GT = gref[pl.ds(pl.multiple_of(n * L, L), L), :].T[:4 * hps, :]
        for d in (0, 1):
            lf = jax.nn.log_sigmoid(GT[(2 + d) * hps:(3 + d) * hps, :])
            trif_t = tris[1 - d].astype(F32)
            b = sum(_dot(part.astype(F32), trif_t) for part in _split3(lf))
            r = GT[d * hps:(d + 1) * hps, :] - b
            b_last = b[:, L - 1:L] if d == 0 else b[:, 0:1]
            wlog = b_last + r
            wide = lambda x: jnp.broadcast_to(x, (hps, L))
            rows_ref[d * nc + slot0 + n] = jnp.concatenate(
                [r, lf, wlog, wide(b_last), wide(jnp.max(wlog, axis=1, keepdims=True)),
                 jnp.zeros((rows_ref.shape[1] - 5 * hps, L), F32)], axis=0)
        return carry

    lax.fori_loop(0, nct, functools.partial(gate_body, gref=gc_ref, slot0=0), 0, unroll=True)
    lax.fori_loop(0, nt, functools.partial(gate_body, gref=g_ref, slot0=nct), 0, unroll=8)

    for d in (0, 1):
        tri = tris[d]
        base = d * nc
        c_ref[...] = jnp.zeros(c_ref.shape, F32)

        def stab_body(n, m_prev, nchunks, slot0, d=d, base=base):
            idx = slot0 + (n if d == 0 else nchunks - 1 - n)
            mprev_ref[idx] = m_prev
            return jnp.maximum(rows_ref[base + idx, R_M:R_M + hps, :] + m_prev,
                               rows_ref[base + idx, R_DEC:R_DEC + hps, :])

        def weight_body(n, carry, base=base):
            R = rows_ref[base + n]
            m_prev = mprev_ref[n]
            bm = R[R_M:R_M + hps, :] + m_prev
            m_new = jnp.maximum(bm, R[R_DEC:R_DEC + hps, :])
            rows_ref[base + n, R_W:R_W + 3 * hps, :] = jnp.concatenate(
                [jnp.exp(R[R_W:R_W + hps, :] - m_new), m_prev, jnp.exp(bm - m_new)], axis=0)
            return carry

        m_run = lax.fori_loop(0, nct, functools.partial(stab_body, nchunks=nct, slot0=0),
                              jnp.zeros((hps, L), F32), unroll=True)
        lax.fori_loop(0, nt, functools.partial(stab_body, nchunks=nt, slot0=nct), m_run, unroll=True)
        lax.fori_loop(0, nct + nt, weight_body, 0, unroll=6)

        def update_state(p, R, rows, ktref, vref):
            kT = ktref[p * LANES:(p + 1) * LANES, rows].astype(F32)
            upd = None
            decay_rows = None
            for a in range(2):
                i = 2 * p + a
                vext = jnp.concatenate([vref[rows, i * LANES:(i + 1) * LANES], ones_ext], axis=1)
                rowmask = (krow < M_QK_DIM) if a == 0 else (krow >= M_QK_DIM)
                kw = jnp.where(rowmask, kT * R[R_W + i:R_W + i + 1, :], 0.0)
                u = _dot(kw.astype(BF16).astype(F32), vext.astype(F32))
                upd = u if upd is None else upd + u
                dr = jnp.where(rowmask, R[R_DEC + i:R_DEC + i + 1, 0:1], 0.0)
                decay_rows = dr if decay_rows is None else decay_rows + dr
            c_ref[p] = decay_rows * c_ref[p] + upd

        def ctx_body(n, carry, d=d, base=base, update_state=update_state):
            c = n if d == 0 else nct - 1 - n
            rows = pl.ds(pl.multiple_of(c * L, L), L)
            for p in range(hps // 2):
                update_state(p, rows_ref[base + c], rows, ktc_ref, vc_ref)
            return carry

        lax.fori_loop(0, nct, ctx_body, 0, unroll=True)

        def head_q(rows, p, a):
            q2 = q_ref[rows, p * LANES:(p + 1) * LANES]
            headmask = (lane < M_QK_DIM) if a == 0 else (lane >= M_QK_DIM)
            return jnp.where(headmask, q2, jnp.zeros_like(q2))

        def pair_weights(c, s_ref, wi_ref, fl_ref, tri=tri, base=base, head_q=head_q):
            rows = pl.ds(pl.multiple_of(c * L, L), L)
            R = rows_ref[base + nct + c]
            for i in range(hps):
                p, a = divmod(i, 2)
                s_raw = _dot(head_q(rows, p, a), kt_ref[p * LANES:(p + 1) * LANES, rows])
                m_prev = R[R_M + i:R_M + i + 1, 0:1]
                rm = jnp.where(tri, R[R_R + i:R_R + i + 1, :], -jnp.inf)
                mu = jnp.maximum(m_prev, jnp.max(rm, axis=1, keepdims=True))
                bcol = jnp.sum(jnp.where(tri, R[R_LF + i:R_LF + i + 1, :], 0.0), axis=1, keepdims=True)
                mu_b = jnp.broadcast_to(mu, (L, L))
                s_ref[i] = s_raw * jnp.exp(rm - mu_b)
                wi_ref[i] = jnp.exp(m_prev - mu_b)
                fl_ref[i] = jnp.exp(-jnp.broadcast_to(bcol + mu, (L, LANES)))

        def emit(c, s_ref, wi_ref, fl_ref, d=d, base=base, head_q=head_q, update_state=update_state):
            rows = pl.ds(pl.multiple_of(c * L, L), L)
            for p in range(hps // 2):
                C2b = c_ref[p].astype(BF16)
                for a in range(2):
                    i = 2 * p + a
                    cs = slice(i * LANES, (i + 1) * LANES)
                    vext = jnp.concatenate([v_ref[rows, cs], ones_ext], axis=1)
                    nd = (jnp.concatenate([wi_ref[i]] * 2, axis=1) * _dot(head_q(rows, p, a), C2b)
                          + _dot(s_ref[i].astype(BF16), vext))
                    h = nd[:, :LANES] / jnp.maximum(jnp.abs(nd[:, LANES:]), fl_ref[i])
                    if d == 0:
                        hacc_ref[rows, cs] = h
                    else:
                        out_ref[rows, cs] = (hacc_ref[rows, cs] + h).astype(BF16)
                update_state(p, rows_ref[base + nct + c], rows, kt_ref, v_ref)

        order = (lambda n: n) if d == 0 else (lambda n: nt - 1 - n)
        slot0 = (s0_ref, wi0_ref, fl0_ref)
        slot1 = (s1_ref, wi1_ref, fl1_ref)

        def lat_body(n, carry, pair_weights=pair_weights, emit=emit, order=order):
            slots = (slot0, slot1)
            for k in range(body_chunks):
                c = body_chunks * n + k
                pair_weights(order(jnp.minimum(c + 1, nt - 1)), *slots[(k + 1) % 2])
                emit(order(c), *slots[k % 2])
            return carry

        pair_weights(order(0), *slot0)
        lax.fori_loop(0, nt // body_chunks, lat_body, 0)


def _mlstm(q, kt, v, g, ktc, vc, gc):
    B, T, _ = q.shape
    Tc = vc.shape[1]
    hps = HEADS_PER_STEP
    G = M_HEADS // hps
    wq, wv = hps * M_QK_DIM, hps * M_V_DIM
    lat = lambda w: pl.BlockSpec((None, T, w), lambda b, g: (b, 0, g))
    cx = lambda w: pl.BlockSpec((None, Tc, w), lambda b, g: (b, 0, g))
    return pl.pallas_call(
        _mlstm_kernel,
        out_shape=jax.ShapeDtypeStruct((B, T, M_HEADS * M_V_DIM), BF16),
        grid=(B, G),
        in_specs=[lat(wq), pl.BlockSpec((wq, T), lambda b, g: (g, b)), lat(wv), lat(LANES),
                  pl.BlockSpec((wq, Tc), lambda b, g: (g, b)), cx(wv), cx(LANES)],
        out_specs=lat(wv),
        scratch_shapes=[pltpu.VMEM((T, wv), F32),
                        pltpu.VMEM((hps // 2, 2 * M_QK_DIM, 2 * LANES), F32),
                        pltpu.VMEM(((T + Tc) // M_CHUNK, hps, M_CHUNK), F32),
                        pltpu.VMEM((2 * (T + Tc) // M_CHUNK, 8 * hps, M_CHUNK), F32)]
                       + [pltpu.VMEM((hps, M_CHUNK, M_CHUNK), F32)] * 6,
        compiler_params=_params(("parallel", "parallel")),
        name="mlstm_scan",
    )(q, kt, v, g, ktc, vc, gc)


def _pivot_rows(b, h, off):
    L, N = b.shape
    bc = lambda r, n: jnp.broadcast_to(b[r:r + 1, :], (n, N))
    if 2 * h >= 8:
        return jnp.concatenate([bc(x + off, 2 * h) for x in range(0, L, 2 * h)], axis=0)
    sub = lax.broadcasted_iota(jnp.int32, (8, N), 0)
    tiles = []
    for x in range(0, L, 8):
        t = bc(x + 8 - 2 * h + off, 8)
        for y in range(8 - 4 * h, -1, -2 * h):
            t = jnp.where(sub < y + 2 * h, bc(x + y + off, 8), t)
        tiles.append(t)
    return jnp.concatenate(tiles, axis=0)


def _block_entry_rows(b, beta, d):
    L, N = b.shape
    tiles = []
    for x in range(0, L, beta):
        r = x - 1 if d == 0 else x + beta
        inside = 0 <= r < L
        tiles.append(jnp.broadcast_to(b[r:r + 1, :], (beta, N)) if inside else jnp.zeros((beta, N), F32))
    return jnp.concatenate(tiles, axis=0)


def _hgrn_kernel(q_ref, kf_ref, kb_ref, v_ref, lff_ref, lfb_ref,
                 kfc_ref, kbc_ref, vc_ref, lffc_ref, lfbc_ref,
                 out_ref, hacc_ref, st_ref, b_ref, bc_ref, attn0_ref, attn1_ref, *, beta):
    L = H_CHUNK
    hps = HEADS_PER_STEP
    T = q_ref.shape[0]
    Tc = vc_ref.shape[0]
    nt, nct = T // L, Tc // L
    body_chunks = min(H_BODY_CHUNKS, nt)
    assert nt % body_chunks == 0 and body_chunks % 2 == 0
    row = lax.broadcasted_iota(jnp.int32, (L, L), 0)
    col = lax.broadcasted_iota(jnp.int32, (L, L), 1)
    halves = [h for h in (L >> (n + 1) for n in range(L.bit_length() - 1)) if h >= beta]
    zpad_f = jnp.zeros((H_DIM - L, H_DIM), F32)
    zpad_b = jnp.zeros((H_DIM - L, H_DIM), BF16)
    neg_log2e = -1.4426950408889634

    for d in (0, 1):
        tri = (col <= row) if d == 0 else (col >= row)
        trib = tri.astype(BF16)
        last = L - 1 if d == 0 else 0
        st_ref[...] = jnp.zeros(st_ref.shape, F32)
        k_lat, lf_lat = (kf_ref, lff_ref) if d == 0 else (kb_ref, lfb_ref)
        k_ctx, lf_ctx = (kfc_ref, lffc_ref) if d == 0 else (kbc_ref, lfbc_ref)
        local = tri & ((row & -beta) == (col & -beta))
        pair = []
        for h in halves:
            same = (row & -(2 * h)) == (col & -(2 * h))
            r_late = (row & (2 * h - 1)) >= h
            c_late = (col & (2 * h - 1)) >= h
            pair.append(same & (r_late & ~c_late if d == 0 else ~r_late & c_late))

        def cumsum_body(n, carry, lfref, bref, trib=trib):
            rows = pl.ds(pl.multiple_of(n * L, L), L)
            bref[rows, :] = _tri_cumsum(trib, lfref[rows, :])
            return carry

        lax.fori_loop(0, nct, functools.partial(cumsum_body, lfref=lf_ctx, bref=bc_ref), 0, unroll=True)
        lax.fori_loop(0, nt, functools.partial(cumsum_body, lfref=lf_lat, bref=b_ref), 0, unroll=8)

        def update_state(i, rows, cs, kref, vref, bref):
            b_h = bref[rows, cs]
            bl = b_h[last:last + 1, :]
            kd = (kref[rows, cs].astype(F32) * jnp.exp(bl - b_h)).astype(BF16)
            vT = jnp.concatenate([vref[rows, cs].astype(F32), zpad_f], axis=0).T.astype(BF16)
            st_ref[i] = (st_ref[i] * jnp.exp(bl)
                         + _dot(vT, jnp.concatenate([kd, zpad_b], axis=0)))

        def ctx_body(n, carry, d=d, k_ctx=k_ctx, update_state=update_state):
            c = n if d == 0 else nct - 1 - n
            rows = pl.ds(pl.multiple_of(c * L, L), L)
            for i in range(hps):
                update_state(i, rows, slice(i * H_DIM, (i + 1) * H_DIM), k_ctx, vc_ref, bc_ref)
            return carry

        lax.fori_loop(0, nct, ctx_body, 0, unroll=True)

        def pair_weights(c, attn_ref, d=d, k_lat=k_lat, local=local, pair=pair):
            rows = pl.ds(pl.multiple_of(c * L, L), L)
            for i in range(hps):
                cs = slice(i * H_DIM, (i + 1) * H_DIM)
                b_h = b_ref[rows, cs]
                q_h = q_ref[rows, cs].astype(F32)
                k_h = k_lat[rows, cs].astype(F32)
                if beta > 1:
                    dl = (b_h - _block_entry_rows(b_h, beta, d)) * neg_log2e
                    qs = (q_h * jnp.exp2(-dl)).astype(BF16)
                    ks = (k_h * jnp.exp2(dl)).astype(BF16)
                else:
                    qs, ks = q_ref[rows, cs], k_lat[rows, cs]
                attn = jnp.where(local, lax.dot_general(qs, ks, _NT, preferred_element_type=F32), 0.0)
                for n, h in enumerate(halves):
                    z = jnp.exp2(jnp.abs(b_h - _pivot_rows(b_h, h, h - 1 + d)) * neg_log2e)
                    qk = lax.dot_general((q_h * z).astype(BF16), (k_h * z).astype(BF16), _NT,
                                         preferred_element_type=F32)
                    attn = jnp.where(pair[n], qk, attn)
                attn_ref[i] = attn

        def emit(c, attn_ref, d=d, k_lat=k_lat, update_state=update_state):
            rows = pl.ds(pl.multiple_of(c * L, L), L)
            for i in range(hps):
                cs = slice(i * H_DIM, (i + 1) * H_DIM)
                qc = (q_ref[rows, cs].astype(F32) * jnp.exp(b_ref[rows, cs])).astype(BF16)
                o = (lax.dot_general(qc, st_ref[i].astype(BF16), _NT, preferred_element_type=F32)
                     + _dot(attn_ref[i].astype(BF16), v_ref[rows, cs]))
                if d == 0:
                    hacc_ref[rows, cs] = o
                else:
                    out_ref[rows, cs] = (hacc_ref[rows, cs] + o).astype(BF16)
                update_state(i, rows, cs, k_lat, v_ref, b_ref)

        order = (lambda n: n) if d == 0 else (lambda n: nt - 1 - n)

        def lat_body(n, carry, pair_weights=pair_weights, emit=emit, order=order):
            slots = (attn0_ref, attn1_ref)
            for k in range(body_chunks):
                c = body_chunks * n + k
                pair_weights(order(jnp.minimum(c + 1, nt - 1)), slots[(k + 1) % 2])
                emit(order(c), slots[k % 2])
            return carry

        pair_weights(order(0), attn0_ref)
        lax.fori_loop(0, nt // body_chunks, lat_body, 0)


def _hgrn(beta, q, kf, kb, v, lff, lfb, kfc, kbc, vc, lffc, lfbc):
    B, T, W = q.shape
    Tc = vc.shape[1]
    hps = HEADS_PER_STEP
    G = H_HEADS // hps
    w = hps * H_DIM
    lat = pl.BlockSpec((None, T, w), lambda b, g: (b, 0, g))
    cx = pl.BlockSpec((None, Tc, w), lambda b, g: (b, 0, g))
    return pl.pallas_call(
        functools.partial(_hgrn_kernel, beta=beta),
        out_shape=jax.ShapeDtypeStruct((B, T, W), BF16),
        grid=(B, G),
        in_specs=[lat] * 6 + [cx] * 5,
        out_specs=lat,
        scratch_shapes=[pltpu.VMEM((T, w), F32),
                        pltpu.VMEM((hps, H_DIM, H_DIM), F32),
                        pltpu.VMEM((T, w), F32),
                        pltpu.VMEM((Tc, w), F32),
                        pltpu.VMEM((hps, H_CHUNK, H_CHUNK), F32),
                        pltpu.VMEM((hps, H_CHUNK, H_CHUNK), F32)],
        compiler_params=_params(("parallel", "parallel")),
        name="hgrn_scan_b%d" % beta,
    )(q, kf, kb, v, lff, lfb, kfc, kbc, vc, lffc, lfbc)


def _to_col_major(a):
    B, T, C = a.shape
    return a.reshape(B, T // GRID_W, GRID_W, C).transpose(0, 2, 1, 3).reshape(B, T, C)


def _to_row_major(a):
    B, T, C = a.shape
    return a.reshape(B, GRID_W, T // GRID_W, C).transpose(0, 2, 1, 3).reshape(B, T, C)


def kernel(x, c, ctx, c_ctx, ada_w, ada_b, ffn1_norm, ffn1_w_in, ffn1_w_out, mix_norm,
           mix_w_in, mix_b_in, mlstm_norm, hgrn_lb_logits, hgrn_norm, proj_m, proj_h,
           mix_w_out, ffn2_norm, ffn2_w_in, ffn2_w_out, final_norm):
    B, T, D = x.shape
    Tc = ctx.shape[1]
    hps = HEADS_PER_STEP
    MQ, MV, HW = M_HEADS * M_QK_DIM, M_HEADS * M_V_DIM, H_HEADS * H_DIM

    pad = (-(B + 1)) % 8
    cc = jnp.concatenate([c, c_ctx[None, :], jnp.zeros((pad, D), F32)], axis=0)
    mod = _modulation(cc, ada_w[0], ada_b[0][None, :])
    ml = mod[:B].reshape(B, N_MOD, D)
    mc = mod[B:B + 1].reshape(1, N_MOD, D)

    row = lambda v: v.reshape(1, -1).astype(F32)
    w1_in = ffn1_w_in[0].astype(BF16)
    w1_out = ffn1_w_out[0].astype(BF16)
    w2_in = ffn2_w_in[0].astype(BF16)
    w2_out = ffn2_w_out[0].astype(BF16)

    W, bias = mix_w_in[0], mix_b_in[0]
    o = 0
    src = {}
    for name, wd in (("mq", MQ), ("mk", MQ), ("mv", MV), ("mo", MV), ("ig", 2 * M_HEADS),
                     ("fg", 2 * M_HEADS), ("hq", HW), ("hff", HW), ("hfb", HW), ("hi", HW),
                     ("hg", HW), ("gm", D), ("gh", D)):
        src[name] = (o, o + wd)
        o += wd
    ws, bs, col, o = [], [], {}, 0
    for name in ("mq", "mk", "mv", "mo", "hq", "hff", "hfb", "hi", "hg", "gm", "gh"):
        a, b = src[name]
        scale = M_QK_DIM ** -0.5 if name == "mq" else 1.0
        ws.append(W[:, a:b] * scale)
        bs.append(bias[a:b] * scale)
        col[name] = o
        o += b - a
    col["gates"] = o
    for g in range(M_HEADS // hps):
        for n in ("ig", "fg"):
            for off in (0, M_HEADS):
                a = src[n][0] + off + g * hps
                ws.append(W[:, a:a + hps])
                bs.append(bias[a:a + hps])
        ws.append(jnp.zeros((D, LANES - 4 * hps), F32))
        bs.append(jnp.zeros((LANES - 4 * hps,), F32))
    w_mix = jnp.concatenate(ws, axis=1).astype(BF16)
    b_mix = jnp.concatenate(bs)[None, :]

    n_gate = LANES * (M_HEADS // hps)
    lb = jnp.cumsum(jax.nn.softmax(hgrn_lb_logits.astype(F32), axis=1), axis=1)[:, 0]

    x2d = x.reshape(B * T, D)
    c2d = ctx.reshape(B * Tc, D)
    tpb = T // TOKEN_TILE
    (xm_c,) = _ffn(c2d, mc, None, row(ffn1_norm[0]), w1_in, w1_out, row(mix_norm[0]),
                   rows=(0, 1, 2), rows2=(3, 4), out_x=False, aux="mix")
    x1, xm_l = _ffn(x2d, ml, tpb, row(ffn1_norm[0]), w1_in, w1_out, row(mix_norm[0]),
                    rows=(0, 1, 2), rows2=(3, 4), out_x=True, aux="mix")

    seg = lambda name, wd, kind, arg=0: (col[name], wd, kind, arg)
    mq, mk, mv, mso, mg, sgm, sgh, hsg = _proj(
        xm_l, w_mix, b_mix, lb,
        [seg("mq", MQ, "bf16"), seg("mk", MQ, "bf16_t"), seg("mv", MV, "bf16"),
         seg("mo", MV, "sigmoid"), seg("gates", n_gate, "f32"), seg("gm", D, "sigmoid"),
         seg("gh", D, "sigmoid"), seg("hg", HW, "silu")], "proj_mlstm")
    mkc, mvc, mgc = _proj(
        xm_c, w_mix, b_mix, lb,
        [seg("mk", MQ, "bf16_t"), seg("mv", MV, "bf16"), seg("gates", n_gate, "f32")],
        "proj_mlstm_ctx")

    xm_cm = _to_col_major(xm_l.reshape(B, T, D)).reshape(B * T, D)
    hq, lff, hkf, lfb, hkb, hv = _proj(
        xm_cm, w_mix, b_mix, lb,
        [seg("hq", HW, "silu"), seg("hff", HW, "hgrn", 0), seg("hfb", HW, "hgrn", 1),
         seg("hi", HW, "bf16")], "proj_hgrn")
    lffc, hkfc, lfbc, hkbc, hvc = _proj(
        xm_c, w_mix, b_mix, lb,
        [seg("hff", HW, "hgrn", 0), seg("hfb", HW, "hgrn", 1), seg("hi", HW, "bf16")],
        "proj_hgrn_ctx")

    r3 = lambda a, t: a.reshape(B, t, a.shape[-1])
    hm = _mlstm(r3(mq, T), mk, r3(mv, T), r3(mg, T), mkc, r3(mvc, Tc), r3(mgc, Tc))
    hgrn_args = (r3(hq, T), r3(hkf, T), r3(hkb, T), r3(hv, T), r3(lff, T), r3(lfb, T),
                 r3(hkfc, Tc), r3(hkbc, Tc), r3(hvc, Tc), r3(lffc, Tc), r3(lfbc, Tc))
    block_ok = H_BLOCK * jnp.max(-jnp.log(lb)) <= H_BLOCK_MAX_EXPONENT
    hh_cm = lax.cond(block_ok, functools.partial(_hgrn, H_BLOCK), functools.partial(_hgrn, 1),
                     *hgrn_args)
    hh = _to_row_major(hh_cm)

    merge = (hm.reshape(B * T, MV), hh.reshape(B * T, HW), mso, hsg, sgm, sgh,
             row(mlstm_norm[0]), row(hgrn_norm[0]),
             proj_m[0].astype(BF16), proj_h[0].astype(BF16), mix_w_out[0].astype(BF16), 5)
    (out,) = _ffn(x1, ml, tpb, row(ffn2_norm[0]), w2_in, w2_out, row(final_norm),
                  rows=(6, 7, 8), rows2=(3, 4), out_x=False, aux="final", merge=merge)
    return out.reshape(B, T, D)
```

```python
import functools

import jax
import jax.numpy as jnp
from jax import lax
from jax.experimental import pallas as pl
from jax.experimental.pallas import tpu as pltpu

F32 = jnp.float32
BF16 = jnp.bfloat16
EPS = 1e-6

GRID_W = 64
M_HEADS = 8
M_QK_DIM = 64
M_V_DIM = 128
H_HEADS = 8
H_DIM = 128
N_MOD = 9

LANES = 128
TOKEN_TILE = 512
FFN_CHUNK = 256
PROJ_CHUNK = 512
M_CHUNK = 128
H_CHUNK = 64
H_BLOCK = 16
H_BLOCK_MAX_EXPONENT = 64.0
HEADS_PER_STEP = 4
M_BODY_CHUNKS = 8
H_BODY_CHUNKS = 16
VMEM_LIMIT = 56 * 1024 * 1024

_NT = (((1,), (1,)), ((), ()))


def _params(sem, vmem=VMEM_LIMIT):
    return pltpu.CompilerParams(dimension_semantics=sem, vmem_limit_bytes=vmem)


def _resident(shape):
    nd = len(shape)
    return pl.BlockSpec(shape, lambda *_: (0,) * nd, pipeline_mode=pl.Buffered(1))


def _dot(a, b):
    return jnp.dot(a, b, preferred_element_type=F32)


def _rms(x, g):
    ms = jnp.mean(x * x, axis=-1, keepdims=True)
    return x * lax.rsqrt(ms + EPS) * g


def _sigmoid(y):
    return 0.5 + 0.5 * jnp.tanh(0.5 * y)


def _split3(x):
    hi = x.astype(BF16)
    r = x - hi.astype(F32)
    mid = r.astype(BF16)
    lo = (r - mid.astype(F32)).astype(BF16)
    return hi, mid, lo


def _tri_cumsum(tri, x):
    hi, mid, _ = _split3(x)
    return _dot(tri, hi) + _dot(tri, mid)


def _mod_kernel(c_ref, w_ref, b_ref, o_ref):
    c = c_ref[...]
    a = c * jax.nn.sigmoid(c)
    a_hi = a.astype(BF16)
    a_lo = (a - a_hi.astype(F32)).astype(BF16)
    w = w_ref[...]
    w_hi = w.astype(BF16)
    w_lo = (w - w_hi.astype(F32)).astype(BF16)
    o_ref[...] = _dot(a_hi, w_hi) + _dot(a_hi, w_lo) + _dot(a_lo, w_hi) + b_ref[...]


def _modulation(cc, w, b):
    R, D = cc.shape
    N = w.shape[1]
    tn = 1024
    return pl.pallas_call(
        _mod_kernel,
        out_shape=jax.ShapeDtypeStruct((R, N), F32),
        grid=(N // tn,),
        in_specs=[pl.BlockSpec((R, D), lambda j: (0, 0)),
                  pl.BlockSpec((D, tn), lambda j: (0, j)),
                  pl.BlockSpec((1, tn), lambda j: (0, j))],
        out_specs=pl.BlockSpec((R, tn), lambda j: (0, j)),
        compiler_params=_params(("arbitrary",)),
        name="modulation",
    )(cc, w, b)


def _ffn_kernel(x_ref, mod_ref, g_ref, win_ref, wout_ref, g2_ref, *rest,
                rows, rows2, out_x, aux, n_chunks, merge_gate_row):
    fc = FFN_CHUNK
    x = x_ref[...]
    mod = mod_ref[0]
    outs = rest
    if merge_gate_row is not None:
        (hm_ref, hh_ref, om_ref, oh_ref, gm_ref, gh_ref, nm_ref, nh_ref,
         pm_ref, ph_ref, wo_ref) = rest[:11]
        outs = rest[11:]

        def branch(h_ref, gate_ref, gain_ref):
            heads = []
            for j in range(h_ref.shape[1] // LANES):
                cs = slice(j * LANES, (j + 1) * LANES)
                heads.append(_rms(h_ref[:, cs].astype(F32), gain_ref[:, cs])
                             * gate_ref[:, cs].astype(F32))
            return jnp.concatenate(heads, axis=1).astype(BF16)

        z = (gm_ref[...].astype(F32) * _dot(branch(hm_ref, om_ref, nm_ref), pm_ref[...])
             + gh_ref[...].astype(F32) * _dot(branch(hh_ref, oh_ref, nh_ref), ph_ref[...])
             ).astype(BF16)
        x = x + mod[merge_gate_row:merge_gate_row + 1] * _dot(z, wo_ref[...])
    shift = mod[rows[0]:rows[0] + 1]
    scale = mod[rows[1]:rows[1] + 1]
    gate = mod[rows[2]:rows[2] + 1]
    xm = (_rms(x, g_ref[...]) * (1.0 + scale) + shift).astype(BF16)
    acc = jnp.zeros(x.shape, F32)
    F = n_chunks * fc
    for j in range(n_chunks):
        a = _dot(xm, win_ref[:, j * fc:(j + 1) * fc])
        u = _dot(xm, win_ref[:, F + j * fc:F + (j + 1) * fc])
        h = (a * jax.nn.sigmoid(a) * u).astype(BF16)
        acc = acc + _dot(h, wout_ref[j * fc:(j + 1) * fc, :])
    xn = x + 0.5 * gate * acc
    k = 0
    if out_x:
        outs[k][...] = xn
        k += 1
    if aux == "mix":
        shift2 = mod[rows2[0]:rows2[0] + 1]
        scale2 = mod[rows2[1]:rows2[1] + 1]
        outs[k][...] = (_rms(xn, g2_ref[...]) * (1.0 + scale2) + shift2).astype(BF16)
    elif aux == "final":
        outs[k][...] = _rms(xn, g2_ref[...])


def _ffn(x2d, mod, tiles_per_mod, g, win, wout, g2, *, rows, rows2, out_x, aux, merge=None):
    N, D = x2d.shape
    F = wout.shape[0]
    tm = TOKEN_TILE
    out_shape, out_specs = [], []
    if out_x:
        out_shape.append(jax.ShapeDtypeStruct((N, D), F32))
        out_specs.append(pl.BlockSpec((tm, D), lambda i: (i, 0)))
    if aux == "mix":
        out_shape.append(jax.ShapeDtypeStruct((N, D), BF16))
        out_specs.append(pl.BlockSpec((tm, D), lambda i: (i, 0)))
    elif aux == "final":
        out_shape.append(jax.ShapeDtypeStruct((N, D), F32))
        out_specs.append(pl.BlockSpec((tm, D), lambda i: (i, 0)))
    if tiles_per_mod is None:
        mod_map = lambda i: (0, 0, 0)
    else:
        mod_map = lambda i: (i // tiles_per_mod, 0, 0)
    kern = functools.partial(_ffn_kernel, rows=rows, rows2=rows2, out_x=out_x, aux=aux,
                             n_chunks=F // FFN_CHUNK,
                             merge_gate_row=None if merge is None else merge[11])
    tile = pl.BlockSpec((tm, D), lambda i: (i, 0))
    in_specs = [tile, pl.BlockSpec((1, N_MOD, D), mod_map), _resident((1, D)),
                _resident(win.shape), _resident(wout.shape), _resident((1, D))]
    args = [x2d, mod, g, win, wout, g2]
    if merge is not None:
        in_specs += [tile] * 6 + [_resident(w.shape) for w in merge[6:11]]
        args += list(merge[:11])
    return pl.pallas_call(
        kern,
        out_shape=out_shape,
        grid=(N // tm,),
        in_specs=in_specs,
        out_specs=out_specs,
        compiler_params=_params(("parallel",)),
        name=("merge_" if merge is not None else "") + "ffn_" + aux,
    )(*args)


def _proj_kernel(x_ref, w_ref, b_ref, lb_ref, *outs, segs):
    x = x_ref[...]
    oi = 0
    for c0, wd, kind, arg in segs:
        for s in range(0, wd, PROJ_CHUNK):
            w = min(PROJ_CHUNK, wd - s)
            y = _dot(x, w_ref[:, c0 + s:c0 + s + w]) + b_ref[:, c0 + s:c0 + s + w]
            if kind == "bf16":
                outs[oi][:, s:s + w] = y.astype(BF16)
            elif kind == "bf16_t":
                outs[oi][s:s + w, :] = y.T.astype(BF16)
            elif kind == "f32":
                outs[oi][:, s:s + w] = y
            elif kind == "sigmoid":
                outs[oi][:, s:s + w] = _sigmoid(y).astype(BF16)
            elif kind == "silu":
                outs[oi][:, s:s + w] = (y * _sigmoid(y)).astype(BF16)
            elif kind == "hgrn":
                lb = lb_ref[arg:arg + 1, s:s + w]
                f = 0.5 * (1.0 + lb) + (0.5 * (1.0 - lb)) * jnp.tanh(0.5 * y)
                outs[oi][:, s:s + w] = jnp.log(f)
                outs[oi + 1][:, s:s + w] = (1.0 - f).astype(BF16)
        oi += 2 if kind == "hgrn" else 1


def _proj(x2d, w, b, lb, segs, name):
    N, D = x2d.shape
    tm = TOKEN_TILE
    out_shape, out_specs = [], []
    for _, wd, kind, _ in segs:
        if kind == "bf16_t":
            out_shape.append(jax.ShapeDtypeStruct((wd, N), BF16))
            out_specs.append(pl.BlockSpec((wd, tm), lambda i: (0, i)))
            continue
        dts = {"bf16": [BF16], "f32": [F32], "sigmoid": [BF16], "silu": [BF16],
               "hgrn": [F32, BF16]}[kind]
        for dt in dts:
            out_shape.append(jax.ShapeDtypeStruct((N, wd), dt))
            out_specs.append(pl.BlockSpec((tm, wd), lambda i: (i, 0)))
    return pl.pallas_call(
        functools.partial(_proj_kernel, segs=tuple(segs)),
        out_shape=out_shape,
        grid=(N // tm,),
        in_specs=[pl.BlockSpec((tm, D), lambda i: (i, 0)),
                  _resident(w.shape),
                  _resident(b.shape),
                  _resident(lb.shape)],
        out_specs=out_specs,
        compiler_params=_params(("parallel",)),
        name=name,
    )(x2d, w, b, lb)


def _mlstm_kernel(q_ref, kt_ref, v_ref, g_ref, ktc_ref, vc_ref, gc_ref,
                  out_ref, hacc_ref, c_ref, mprev_ref, rows_ref,
                  s0_ref, wi0_ref, fl0_ref, s1_ref, wi1_ref, fl1_ref):
    L = M_CHUNK
    hps = HEADS_PER_STEP
    T = q_ref.shape[0]
    Tc = vc_ref.shape[0]
    nt, nct = T // L, Tc // L
    body_chunks = min(M_BODY_CHUNKS, nt)
    assert nt % body_chunks == 0 and body_chunks % 2 == 0
    row = lax.broadcasted_iota(jnp.int32, (L, L), 0)
    col = lax.broadcasted_iota(jnp.int32, (L, L), 1)
    lane = lax.broadcasted_iota(jnp.int32, (L, LANES), 1)
    krow = lax.broadcasted_iota(jnp.int32, (LANES, 1), 0)
    ones_ext = jnp.ones((L, LANES), BF16)
    R_R, R_LF, R_W, R_M, R_DEC = (k * hps for k in range(5))

    nc = nct + nt
    tris = [(col <= row), (col >= row)]

    def gate_body(n, carry, gref, slot0):
        GT = gref[pl.ds(pl.multiple_of(n * L, L), L), :].T[:4 * hps, :]
        for d in (0, 1):
            lf = jax.nn.log_sigmoid(GT[(2 + d) * hps:(3 + d) * hps, :])
            trif_t = tris[1 - d].astype(F32)
            b = sum(_dot(part.astype(F32), trif_t) for part in _split3(lf))
            r = GT[d * hps:(d + 1) * hps, :] - b
            b_last = b[:, L - 1:L] if d == 0 else b[:, 0:1]
            wlog = b_last + r
            wide = lambda x: jnp.broadcast_to(x, (hps, L))
            rows_ref[d * nc + slot0 + n] = jnp.concatenate(
                [r, lf, wlog, wide(b_last), wide(jnp.max(wlog, axis=1, keepdims=True)),
                 jnp.zeros((rows_ref.shape[1] - 5 * hps, L), F32)], axis=0)
        return carry

    lax.fori_loop(0, nct, functools.partial(gate_body, gref=gc_ref, slot0=0), 0, unroll=True)
    lax.fori_loop(0, nt, functools.partial(gate_body, gref=g_ref, slot0=nct), 0, unroll=8)

    for d in (0, 1):
        tri = tris[d]
        base = d * nc
        c_ref[...] = jnp.zeros(c_ref.shape, F32)

        def stab_body(n, m_prev, nchunks, slot0, d=d, base=base):
            idx = slot0 + (n if d == 0 else nchunks - 1 - n)
            mprev_ref[idx] = m_prev
            return jnp.maximum(rows_ref[base + idx, R_M:R_M + hps, :] + m_prev,
                               rows_ref[base + idx, R_DEC:R_DEC + hps, :])

        def weight_body(n, carry, base=base):
            R = rows_ref[base + n]
            m_prev = mprev_ref[n]
            bm = R[R_M:R_M + hps, :] + m_prev
            m_new = jnp.maximum(bm, R[R_DEC:R_DEC + hps, :])
            rows_ref[base + n, R_W:R_W + 3 * hps, :] = jnp.concatenate(
                [jnp.exp(R[R_W:R_W + hps, :] - m_new), m_prev, jnp.exp(bm - m_new)], axis=0)
            return carry

        m_run = lax.fori_loop(0, nct, functools.partial(stab_body, nchunks=nct, slot0=0),
                              jnp.zeros((hps, L), F32), unroll=True)
        lax.fori_loop(0, nt, functools.partial(stab_body, nchunks=nt, slot0=nct), m_run, unroll=True)
        lax.fori_loop(0, nct + nt, weight_body, 0, unroll=6)

        def update_state(p, R, rows, ktref, vref):
            kT = ktref[p * LANES:(p + 1) * LANES, rows].astype(F32)
            upd = None
            decay_rows = None
            for a in range(2):
                i = 2 * p + a
                vext = jnp.concatenate([vref[rows, i * LANES:(i + 1) * LANES], ones_ext], axis=1)
                rowmask = (krow < M_QK_DIM) if a == 0 else (krow >= M_QK_DIM)
                kw = jnp.where(rowmask, kT * R[R_W + i:R_W + i + 1, :], 0.0)
                u = _dot(kw.astype(BF16).astype(F32), vext.astype(F32))
                upd = u if upd is None else upd + u
                dr = jnp.where(rowmask, R[R_DEC + i:R_DEC + i + 1, 0:1], 0.0)
                decay_rows = dr if decay_rows is None else decay_rows + dr
            c_ref[p] = decay_rows * c_ref[p] + upd

        def ctx_body(n, carry, d=d, base=base, update_state=update_state):
            c = n if d == 0 else nct - 1 - n
            rows = pl.ds(pl.multiple_of(c * L, L), L)
            for p in range(hps // 2):
                update_state(p, rows_ref[base + c], rows, ktc_ref, vc_ref)
            return carry

        lax.fori_loop(0, nct, ctx_body, 0, unroll=True)

        def head_q(rows, p, a):
            q2 = q_ref[rows, p * LANES:(p + 1) * LANES]
            headmask = (lane < M_QK_DIM) if a == 0 else (lane >= M_QK_DIM)
            return jnp.where(headmask, q2, jnp.zeros_like(q2))

        def pair_weights(c, s_ref, wi_ref, fl_ref, tri=tri, base=base, head_q=head_q):
            rows = pl.ds(pl.multiple_of(c * L, L), L)
            R = rows_ref[base + nct + c]
            for i in range(hps):
                p, a = divmod(i, 2)
                s_raw = _dot(head_q(rows, p, a), kt_ref[p * LANES:(p + 1) * LANES, rows])
                m_prev = R[R_M + i:R_M + i + 1, 0:1]
                rm = jnp.where(tri, R[R_R + i:R_R + i + 1, :], -jnp.inf)
                mu = jnp.maximum(m_prev, jnp.max(rm, axis=1, keepdims=True))
                bcol = jnp.sum(jnp.where(tri, R[R_LF + i:R_LF + i + 1, :], 0.0), axis=1, keepdims=True)
                mu_b = jnp.broadcast_to(mu, (L, L))
                s_ref[i] = s_raw * jnp.exp(rm - mu_b)
                wi_ref[i] = jnp.exp(m_prev - mu_b)
                fl_ref[i] = jnp.exp(-jnp.broadcast_to(bcol + mu, (L, LANES)))

        def emit(c, s_ref, wi_ref, fl_ref, d=d, base=base, head_q=head_q, update_state=update_state):
            rows = pl.ds(pl.multiple_of(c * L, L), L)
            for p in range(hps // 2):
                C2b = c_ref[p].astype(BF16)
                for a in range(2):
                    i = 2 * p + a
                    cs = slice(i * LANES, (i + 1) * LANES)
                    vext = jnp.concatenate([v_ref[rows, cs], ones_ext], axis=1)
                    nd = (jnp.concatenate([wi_ref[i]] * 2, axis=1) * _dot(head_q(rows, p, a), C2b)
                          + _dot(s_ref[i].astype(BF16), vext))
                    h = nd[:, :LANES] / jnp.maximum(jnp.abs(nd[:, LANES:]), fl_ref[i])
                    if d == 0:
                        hacc_ref[rows, cs] = h
                    else:
                        out_ref[rows, cs] = (hacc_ref[rows, cs] + h).astype(BF16)
                update_state(p, rows_ref[base + nct + c], rows, kt_ref, v_ref)

        order = (lambda n: n) if d == 0 else (lambda n: nt - 1 - n)
        slot0 = (s0_ref, wi0_ref, fl0_ref)
        slot1 = (s1_ref, wi1_ref, fl1_ref)

        def lat_body(n, carry, pair_weights=pair_weights, emit=emit, order=order):
            slots = (slot0, slot1)
            for k in range(body_chunks):
                c = body_chunks * n + k
                pair_weights(order(jnp.minimum(c + 1, nt - 1)), *slots[(k + 1) % 2])
                emit(order(c), *slots[k % 2])
            return carry

        pair_weights(order(0), *slot0)
        lax.fori_loop(0, nt // body_chunks, lat_body, 0)


def _mlstm(q, kt, v, g, ktc, vc, gc):
    B, T, _ = q.shape
    Tc = vc.shape[1]
    hps = HEADS_PER_STEP
    G = M_HEADS // hps
    wq, wv = hps * M_QK_DIM, hps * M_V_DIM
    lat = lambda w: pl.BlockSpec((None, T, w), lambda b, g: (b, 0, g))
    cx = lambda w: pl.BlockSpec((None, Tc, w), lambda b, g: (b, 0, g))
    return pl.pallas_call(
        _mlstm_kernel,
        out_shape=jax.ShapeDtypeStruct((B, T, M_HEADS * M_V_DIM), BF16),
        grid=(B, G),
        in_specs=[lat(wq), pl.BlockSpec((wq, T), lambda b, g: (g, b)), lat(wv), lat(LANES),
                  pl.BlockSpec((wq, Tc), lambda b, g: (g, b)), cx(wv), cx(LANES)],
        out_specs=lat(wv),
        scratch_shapes=[pltpu.VMEM((T, wv), F32),
                        pltpu.VMEM((hps // 2, 2 * M_QK_DIM, 2 * LANES), F32),
                        pltpu.VMEM(((T + Tc) // M_CHUNK, hps, M_CHUNK), F32),
                        pltpu.VMEM((2 * (T + Tc) // M_CHUNK, 8 * hps, M_CHUNK), F32)]
                       + [pltpu.VMEM((hps, M_CHUNK, M_CHUNK), F32)] * 6,
        compiler_params=_params(("parallel", "parallel")),
        name="mlstm_scan",
    )(q, kt, v, g, ktc, vc, gc)


def _pivot_rows(b, h, off):
    L, N = b.shape
    bc = lambda r, n: jnp.broadcast_to(b[r:r + 1, :], (n, N))
    if 2 * h >= 8:
        return jnp.concatenate([bc(x + off, 2 * h) for x in range(0, L, 2 * h)], axis=0)
    sub = lax.broadcasted_iota(jnp.int32, (8, N), 0)
    tiles = []
    for x in range(0, L, 8):
        t = bc(x + 8 - 2 * h + off, 8)
        for y in range(8 - 4 * h, -1, -2 * h):
            t = jnp.where(sub < y + 2 * h, bc(x + y + off, 8), t)
        tiles.append(t)
    return jnp.concatenate(tiles, axis=0)


def _block_entry_rows(b, beta, d):
    L, N = b.shape
    tiles = []
    for x in range(0, L, beta):
        r = x - 1 if d == 0 else x + beta
        inside = 0 <= r < L
        tiles.append(jnp.broadcast_to(b[r:r + 1, :], (beta, N)) if inside else jnp.zeros((beta, N), F32))
    return jnp.concatenate(tiles, axis=0)


def _hgrn_kernel(q_ref, kf_ref, kb_ref, v_ref, lff_ref, lfb_ref,
                 kfc_ref, kbc_ref, vc_ref, lffc_ref, lfbc_ref,
                 out_ref, hacc_ref, st_ref, b_ref, bc_ref, attn0_ref, attn1_ref, *, beta):
    L = H_CHUNK
    hps = HEADS_PER_STEP
    T = q_ref.shape[0]
    Tc = vc_ref.shape[0]
    nt, nct = T // L, Tc // L
    body_chunks = min(H_BODY_CHUNKS, nt)
    assert nt % body_chunks == 0 and body_chunks % 2 == 0
    row = lax.broadcasted_iota(jnp.int32, (L, L), 0)
    col = lax.broadcasted_iota(jnp.int32, (L, L), 1)
    halves = [h for h in (L >> (n + 1) for n in range(L.bit_length() - 1)) if h >= beta]
    zpad_f = jnp.zeros((H_DIM - L, H_DIM), F32)
    zpad_b = jnp.zeros((H_DIM - L, H_DIM), BF16)
    neg_log2e = -1.4426950408889634

    for d in (0, 1):
        tri = (col <= row) if d == 0 else (col >= row)
        trib = tri.astype(BF16)
        last = L - 1 if d == 0 else 0
        st_ref[...] = jnp.zeros(st_ref.shape, F32)
        k_lat, lf_lat = (kf_ref, lff_ref) if d == 0 else (kb_ref, lfb_ref)
        k_ctx, lf_ctx = (kfc_ref, lffc_ref) if d == 0 else (kbc_ref, lfbc_ref)
        local = tri & ((row & -beta) == (col & -beta))
        pair = []
        for h in halves:
            same = (row & -(2 * h)) == (col & -(2 * h))
            r_late = (row & (2 * h - 1)) >= h
            c_late = (col & (2 * h - 1)) >= h
            pair.append(same & (r_late & ~c_late if d == 0 else ~r_late & c_late))

        def cumsum_body(n, carry, lfref, bref, trib=trib):
            rows = pl.ds(pl.multiple_of(n * L, L), L)
            bref[rows, :] = _tri_cumsum(trib, lfref[rows, :])
            return carry

        lax.fori_loop(0, nct, functools.partial(cumsum_body, lfref=lf_ctx, bref=bc_ref), 0, unroll=True)
        lax.fori_loop(0, nt, functools.partial(cumsum_body, lfref=lf_lat, bref=b_ref), 0, unroll=8)

        def update_state(i, rows, cs, kref, vref, bref):
            b_h = bref[rows, cs]
            bl = b_h[last:last + 1, :]
            kd = (kref[rows, cs].astype(F32) * jnp.exp(bl - b_h)).astype(BF16)
            vT = jnp.concatenate([vref[rows, cs].astype(F32), zpad_f], axis=0).T.astype(BF16)
            st_ref[i] = (st_ref[i] * jnp.exp(bl)
                         + _dot(vT, jnp.concatenate([kd, zpad_b], axis=0)))

        def ctx_body(n, carry, d=d, k_ctx=k_ctx, update_state=update_state):
            c = n if d == 0 else nct - 1 - n
            rows = pl.ds(pl.multiple_of(c * L, L), L)
            for i in range(hps):
                update_state(i, rows, slice(i * H_DIM, (i + 1) * H_DIM), k_ctx, vc_ref, bc_ref)
            return carry

        lax.fori_loop(0, nct, ctx_body, 0, unroll=True)

        def pair_weights(c, attn_ref, d=d, k_lat=k_lat, local=local, pair=pair):
            rows = pl.ds(pl.multiple_of(c * L, L), L)
            for i in range(hps):
                cs = slice(i * H_DIM, (i + 1) * H_DIM)
                b_h = b_ref[rows, cs]
                q_h = q_ref[rows, cs].astype(F32)
                k_h = k_lat[rows, cs].astype(F32)
                if beta > 1:
                    dl = (b_h - _block_entry_rows(b_h, beta, d)) * neg_log2e
                    qs = (q_h * jnp.exp2(-dl)).astype(BF16)
                    ks = (k_h * jnp.exp2(dl)).astype(BF16)
                else:
                    qs, ks = q_ref[rows, cs], k_lat[rows, cs]
                attn = jnp.where(local, lax.dot_general(qs, ks, _NT, preferred_element_type=F32), 0.0)
                for n, h in enumerate(halves):
                    z = jnp.exp2(jnp.abs(b_h - _pivot_rows(b_h, h, h - 1 + d)) * neg_log2e)
                    qk = lax.dot_general((q_h * z).astype(BF16), (k_h * z).astype(BF16), _NT,
                                         preferred_element_type=F32)
                    attn = jnp.where(pair[n], qk, attn)
                attn_ref[i] = attn

        def emit(c, attn_ref, d=d, k_lat=k_lat, update_state=update_state):
            rows = pl.ds(pl.multiple_of(c * L, L), L)
            for i in range(hps):
                cs = slice(i * H_DIM, (i + 1) * H_DIM)
                qc = (q_ref[rows, cs].astype(F32) * jnp.exp(b_ref[rows, cs])).astype(BF16)
                o = (lax.dot_general(qc, st_ref[i].astype(BF16), _NT, preferred_element_type=F32)
                     + _dot(attn_ref[i].astype(BF16), v_ref[rows, cs]))
                if d == 0:
                    hacc_ref[rows, cs] = o
                else:
                    out_ref[rows, cs] = (hacc_ref[rows, cs] + o).astype(BF16)
                update_state(i, rows, cs, k_lat, v_ref, b_ref)

        order = (lambda n: n) if d == 0 else (lambda n: nt - 1 - n)

        def lat_body(n, carry, pair_weights=pair_weights, emit=emit, order=order):
            slots = (attn0_ref, attn1_ref)
            for k in range(body_chunks):
                c = body_chunks * n + k
                pair_weights(order(jnp.minimum(c + 1, nt - 1)), slots[(k + 1) % 2])
                emit(order(c), slots[k % 2])
            return carry

        pair_weights(order(0), attn0_ref)
        lax.fori_loop(0, nt // body_chunks, lat_body, 0)


def _hgrn(beta, q, kf, kb, v, lff, lfb, kfc, kbc, vc, lffc, lfbc):
    B, T, W = q.shape
    Tc = vc.shape[1]
    hps = HEADS_PER_STEP
    G = H_HEADS // hps
    w = hps * H_DIM
    lat = pl.BlockSpec((None, T, w), lambda b, g: (b, 0, g))
    cx = pl.BlockSpec((None, Tc, w), lambda b, g: (b, 0, g))
    return pl.pallas_call(
        functools.partial(_hgrn_kernel, beta=beta),
        out_shape=jax.ShapeDtypeStruct((B, T, W), BF16),
        grid=(B, G),
        in_specs=[lat] * 6 + [cx] * 5,
        out_specs=lat,
        scratch_shapes=[pltpu.VMEM((T, w), F32),
                        pltpu.VMEM((hps, H_DIM, H_DIM), F32),
                        pltpu.VMEM((T, w), F32),
                        pltpu.VMEM((Tc, w), F32),
                        pltpu.VMEM((hps, H_CHUNK, H_CHUNK), F32),
                        pltpu.VMEM((hps, H_CHUNK, H_CHUNK), F32)],
        compiler_params=_params(("parallel", "parallel")),
        name="hgrn_scan_b%d" % beta,
    )(q, kf, kb, v, lff, lfb, kfc, kbc, vc, lffc, lfbc)


def _to_col_major(a):
    B, T, C = a.shape
    return a.reshape(B, T // GRID_W, GRID_W, C).transpose(0, 2, 1, 3).reshape(B, T, C)


def _to_row_major(a):
    B, T, C = a.shape
    return a.reshape(B, GRID_W, T // GRID_W, C).transpose(0, 2, 1, 3).reshape(B, T, C)


def kernel(x, c, ctx, c_ctx, ada_w, ada_b, ffn1_norm, ffn1_w_in, ffn1_w_out, mix_norm,
           mix_w_in, mix_b_in, mlstm_norm, hgrn_lb_logits, hgrn_norm, proj_m, proj_h,
           mix_w_out, ffn2_norm, ffn2_w_in, ffn2_w_out, final_norm):
    B, T, D = x.shape
    Tc = ctx.shape[1]
    hps = HEADS_PER_STEP
    MQ, MV, HW = M_HEADS * M_QK_DIM, M_HEADS * M_V_DIM, H_HEADS * H_DIM

    pad = (-(B + 1)) % 8
    cc = jnp.concatenate([c, c_ctx[None, :], jnp.zeros((pad, D), F32)], axis=0)
    mod = _modulation(cc, ada_w[0], ada_b[0][None, :])
    ml = mod[:B].reshape(B, N_MOD, D)
    mc = mod[B:B + 1].reshape(1, N_MOD, D)

    row = lambda v: v.reshape(1, -1).astype(F32)
    w1_in = ffn1_w_in[0].astype(BF16)
    w1_out = ffn1_w_out[0].astype(BF16)
    w2_in = ffn2_w_in[0].astype(BF16)
    w2_out = ffn2_w_out[0].astype(BF16)

    W, bias = mix_w_in[0], mix_b_in[0]
    o = 0
    src = {}
    for name, wd in (("mq", MQ), ("mk", MQ), ("mv", MV), ("mo", MV), ("ig", 2 * M_HEADS),
                     ("fg", 2 * M_HEADS), ("hq", HW), ("hff", HW), ("hfb", HW), ("hi", HW),
                     ("hg", HW), ("gm", D), ("gh", D)):
        src[name] = (o, o + wd)
        o += wd
    ws, bs, col, o = [], [], {}, 0
    for name in ("mq", "mk", "mv", "mo", "hq", "hff", "hfb", "hi", "hg", "gm", "gh"):
        a, b = src[name]
        scale = M_QK_DIM ** -0.5 if name == "mq" else 1.0
        ws.append(W[:, a:b] * scale)
        bs.append(bias[a:b] * scale)
        col[name] = o
        o += b - a
    col["gates"] = o
    for g in range(M_HEADS // hps):
        for n in ("ig", "fg"):
            for off in (0, M_HEADS):
                a = src[n][0] + off + g * hps
                ws.append(W[:, a:a + hps])
                bs.append(bias[a:a + hps])
        ws.append(jnp.zeros((D, LANES - 4 * hps), F32))
        bs.append(jnp.zeros((LANES - 4 * hps,), F32))
    w_mix = jnp.concatenate(ws, axis=1).astype(BF16)
    b_mix = jnp.concatenate(bs)[None, :]

    n_gate = LANES * (M_HEADS // hps)
    lb = jnp.cumsum(jax.nn.softmax(hgrn_lb_logits.astype(F32), axis=1), axis=1)[:, 0]

    x2d = x.reshape(B * T, D)
    c2d = ctx.reshape(B * Tc, D)
    tpb = T // TOKEN_TILE
    (xm_c,) = _ffn(c2d, mc, None, row(ffn1_norm[0]), w1_in, w1_out, row(mix_norm[0]),
                   rows=(0, 1, 2), rows2=(3, 4), out_x=False, aux="mix")
    x1, xm_l = _ffn(x2d, ml, tpb, row(ffn1_norm[0]), w1_in, w1_out, row(mix_norm[0]),
                    rows=(0, 1, 2), rows2=(3, 4), out_x=True, aux="mix")

    seg = lambda name, wd, kind, arg=0: (col[name], wd, kind, arg)
    mq, mk, mv, mso, mg, sgm, sgh, hsg = _proj(
        xm_l, w_mix, b_mix, lb,
        [seg("mq", MQ, "bf16"), seg("mk", MQ, "bf16_t"), seg("mv", MV, "bf16"),
         seg("mo", MV, "sigmoid"), seg("gates", n_gate, "f32"), seg("gm", D, "sigmoid"),
         seg("gh", D, "sigmoid"), seg("hg", HW, "silu")], "proj_mlstm")
    mkc, mvc, mgc = _proj(
        xm_c, w_mix, b_mix, lb,
        [seg("mk", MQ, "bf16_t"), seg("mv", MV, "bf16"), seg("gates", n_gate, "f32")],
        "proj_mlstm_ctx")

    xm_cm = _to_col_major(xm_l.reshape(B, T, D)).reshape(B * T, D)
    hq, lff, hkf, lfb, hkb, hv = _proj(
        xm_cm, w_mix, b_mix, lb,
        [seg("hq", HW, "silu"), seg("hff", HW, "hgrn", 0), seg("hfb", HW, "hgrn", 1),
         seg("hi", HW, "bf16")], "proj_hgrn")
    lffc, hkfc, lfbc, hkbc, hvc = _proj(
        xm_c, w_mix, b_mix, lb,
        [seg("hff", HW, "hgrn", 0), seg("hfb", HW, "hgrn", 1), seg("hi", HW, "bf16")],
        "proj_hgrn_ctx")

    r3 = lambda a, t: a.reshape(B, t, a.shape[-1])
    hm = _mlstm(r3(mq, T), mk, r3(mv, T), r3(mg, T), mkc, r3(mvc, Tc), r3(mgc, Tc))
    hgrn_args = (r3(hq, T), r3(hkf, T), r3(hkb, T), r3(hv, T), r3(lff, T), r3(lfb, T),
                 r3(hkfc, Tc), r3(hkbc, Tc), r3(hvc, Tc), r3(lffc, Tc), r3(lfbc, Tc))
    block_ok = H_BLOCK * jnp.max(-jnp.log(lb)) <= H_BLOCK_MAX_EXPONENT
    hh_cm = lax.cond(block_ok, functools.partial(_hgrn, H_BLOCK), functools.partial(_hgrn, 1),
                     *hgrn_args)
    hh = _to_row_major(hh_cm)

    merge = (hm.reshape(B * T, MV), hh.reshape(B * T, HW), mso, hsg, sgm, sgh,
             row(mlstm_norm[0]), row(hgrn_norm[0]),
             proj_m[0].astype(BF16), proj_h[0].astype(BF16), mix_w_out[0].astype(BF16), 5)
    (out,) = _ffn(x1, ml, tpb, row(ffn2_norm[0]), w2_in, w2_out, row(final_norm),
                  rows=(6, 7, 8), rows2=(3, 4), out_x=False, aux="final", merge=merge)
    return out.reshape(B, T, D)
```

```python
import functools

import jax
import jax.numpy as jnp
from jax import lax
from jax.experimental import pallas as pl
from jax.experimental.pallas import tpu as pltpu

F32 = jnp.float32
BF16 = jnp.bfloat16
EPS = 1e-6

GRID_W = 64
M_HEADS = 8
M_QK_DIM = 64
M_V_DIM = 128
H_HEADS = 8
H_DIM = 128
N_MOD = 9

LANES = 128
TOKEN_TILE = 512
FFN_CHUNK = 256
PROJ_CHUNK = 512
M_CHUNK = 128
H_CHUNK = 64
H_BLOCK = 16
H_BLOCK_MAX_EXPONENT = 64.0
HEADS_PER_STEP = 4
M_BODY_CHUNKS = 8
H_BODY_CHUNKS = 16
VMEM_LIMIT = 56 * 1024 * 1024

_NT = (((1,), (1,)), ((), ()))


def _params(sem, vmem=VMEM_LIMIT):
    return pltpu.CompilerParams(dimension_semantics=sem, vmem_limit_bytes=vmem)


def _resident(shape):
    nd = len(shape)
    return pl.BlockSpec(shape, lambda *_: (0,) * nd, pipeline_mode=pl.Buffered(1))


def _dot(a, b):
    return jnp.dot(a, b, preferred_element_type=F32)


def _rms(x, g):
    ms = jnp.mean(x * x, axis=-1, keepdims=True)
    return x * lax.rsqrt(ms + EPS) * g


def _sigmoid(y):
    return 0.5 + 0.5 * jnp.tanh(0.5 * y)


def _split3(x):
    hi = x.astype(BF16)
    r = x - hi.astype(F32)
    mid = r.astype(BF16)
    lo = (r - mid.astype(F32)).astype(BF16)
    return hi, mid, lo


def _tri_cumsum(tri, x):
    hi, mid, _ = _split3(x)
    return _dot(tri, hi) + _dot(tri, mid)


def _mod_kernel(c_ref, w_ref, b_ref, o_ref):
    c = c_ref[...]
    a = c * jax.nn.sigmoid(c)
    a_hi = a.astype(BF16)
    a_lo = (a - a_hi.astype(F32)).astype(BF16)
    w = w_ref[...]
    w_hi = w.astype(BF16)
    w_lo = (w - w_hi.astype(F32)).astype(BF16)
    o_ref[...] = _dot(a_hi, w_hi) + _dot(a_hi, w_lo) + _dot(a_lo, w_hi) + b_ref[...]


def _modulation(cc, w, b):
    R, D = cc.shape
    N = w.shape[1]
    tn = 1024
    return pl.pallas_call(
        _mod_kernel,
        out_shape=jax.ShapeDtypeStruct((R, N), F32),
        grid=(N // tn,),
        in_specs=[pl.BlockSpec((R, D), lambda j: (0, 0)),
                  pl.BlockSpec((D, tn), lambda j: (0, j)),
                  pl.BlockSpec((1, tn), lambda j: (0, j))],
        out_specs=pl.BlockSpec((R, tn), lambda j: (0, j)),
        compiler_params=_params(("arbitrary",)),
        name="modulation",
    )(cc, w, b)


def _ffn_kernel(x_ref, mod_ref, g_ref, win_ref, wout_ref, g2_ref, *rest,
                rows, rows2, out_x, aux, n_chunks, merge_gate_row):
    fc = FFN_CHUNK
    x = x_ref[...]
    mod = mod_ref[0]
    outs = rest
    if merge_gate_row is not None:
        (hm_ref, hh_ref, om_ref, oh_ref, gm_ref, gh_ref, nm_ref, nh_ref,
         pm_ref, ph_ref, wo_ref) = rest[:11]
        outs = rest[11:]

        def branch(h_ref, gate_ref, gain_ref):
            heads = []
            for j in range(h_ref.shape[1] // LANES):
                cs = slice(j * LANES, (j + 1) * LANES)
                heads.append(_rms(h_ref[:, cs].astype(F32), gain_ref[:, cs])
                             * gate_ref[:, cs].astype(F32))
            return jnp.concatenate(heads, axis=1).astype(BF16)

        z = (gm_ref[...].astype(F32) * _dot(branch(hm_ref, om_ref, nm_ref), pm_ref[...])
             + gh_ref[...].astype(F32) * _dot(branch(hh_ref, oh_ref, nh_ref), ph_ref[...])
             ).astype(BF16)
        x = x + mod[merge_gate_row:merge_gate_row + 1] * _dot(z, wo_ref[...])
    shift = mod[rows[0]:rows[0] + 1]
    scale = mod[rows[1]:rows[1] + 1]
    gate = mod[rows[2]:rows[2] + 1]
    xm = (_rms(x, g_ref[...]) * (1.0 + scale) + shift).astype(BF16)
    acc = jnp.zeros(x.shape, F32)
    F = n_chunks * fc
    for j in range(n_chunks):
        a = _dot(xm, win_ref[:, j * fc:(j + 1) * fc])
        u = _dot(xm, win_ref[:, F + j * fc:F + (j + 1) * fc])
        h = (a * jax.nn.sigmoid(a) * u).astype(BF16)
        acc = acc + _dot(h, wout_ref[j * fc:(j + 1) * fc, :])
    xn = x + 0.5 * gate * acc
    k = 0
    if out_x:
        outs[k][...] = xn
        k += 1
    if aux == "mix":
        shift2 = mod[rows2[0]:rows2[0] + 1]
        scale2 = mod[rows2[1]:rows2[1] + 1]
        outs[k][...] = (_rms(xn, g2_ref[...]) * (1.0 + scale2) + shift2).astype(BF16)
    elif aux == "final":
        outs[k][...] = _rms(xn, g2_ref[...])


def _ffn(x2d, mod, tiles_per_mod, g, win, wout, g2, *, rows, rows2, out_x, aux, merge=None):
    N, D = x2d.shape
    F = wout.shape[0]
    tm = TOKEN_TILE
    out_shape, out_specs = [], []
    if out_x:
        out_shape.append(jax.ShapeDtypeStruct((N, D), F32))
        out_specs.append(pl.BlockSpec((tm, D), lambda i: (i, 0)))
    if aux == "mix":
        out_shape.append(jax.ShapeDtypeStruct((N, D), BF16))
        out_specs.append(pl.BlockSpec((tm, D), lambda i: (i, 0)))
    elif aux == "final":
        out_shape.append(jax.ShapeDtypeStruct((N, D), F32))
        out_specs.append(pl.BlockSpec((tm, D), lambda i: (i, 0)))
    if tiles_per_mod is None:
        mod_map = lambda i: (0, 0, 0)
    else:
        mod_map = lambda i: (i // tiles_per_mod, 0, 0)
    kern = functools.partial(_ffn_kernel, rows=rows, rows2=rows2, out_x=out_x, aux=aux,
                             n_chunks=F // FFN_CHUNK,
                             merge_gate_row=None if merge is None else merge[11])
    tile = pl.BlockSpec((tm, D), lambda i: (i, 0))
    in_specs = [tile, pl.BlockSpec((1, N_MOD, D), mod_map), _resident((1, D)),
                _resident(win.shape), _resident(wout.shape), _resident((1, D))]
    args = [x2d, mod, g, win, wout, g2]
    if merge is not None:
        in_specs += [tile] * 6 + [_resident(w.shape) for w in merge[6:11]]
        args += list(merge[:11])
    return pl.pallas_call(
        kern,
        out_shape=out_shape,
        grid=(N // tm,),
        in_specs=in_specs,
        out_specs=out_specs,
        compiler_params=_params(("parallel",)),
        name=("merge_" if merge is not None else "") + "ffn_" + aux,
    )(*args)


def _proj_kernel(x_ref, w_ref, b_ref, lb_ref, *outs, segs):
    x = x_ref[...]
    oi = 0
    for c0, wd, kind, arg in segs:
        for s in range(0, wd, PROJ_CHUNK):
            w = min(PROJ_CHUNK, wd - s)
            y = _dot(x, w_ref[:, c0 + s:c0 + s + w]) + b_ref[:, c0 + s:c0 + s + w]
            if kind == "bf16":
                outs[oi][:, s:s + w] = y.astype(BF16)
            elif kind == "bf16_t":
                outs[oi][s:s + w, :] = y.T.astype(BF16)
            elif kind == "f32":
                outs[oi][:, s:s + w] = y
            elif kind == "sigmoid":
                outs[oi][:, s:s + w] = _sigmoid(y).astype(BF16)
            elif kind == "silu":
                outs[oi][:, s:s + w] = (y * _sigmoid(y)).astype(BF16)
            elif kind == "hgrn":
                lb = lb_ref[arg:arg + 1, s:s + w]
                f = 0.5 * (1.0 + lb) + (0.5 * (1.0 - lb)) * jnp.tanh(0.5 * y)
                outs[oi][:, s:s + w] = jnp.log(f)
                outs[oi + 1][:, s:s + w] = (1.0 - f).astype(BF16)
        oi += 2 if kind == "hgrn" else 1


def _proj(x2d, w, b, lb, segs, name):
    N, D = x2d.shape
    tm = TOKEN_TILE
    out_shape, out_specs = [], []
    for _, wd, kind, _ in segs:
        if kind == "bf16_t":
            out_shape.append(jax.ShapeDtypeStruct((wd, N), BF16))
            out_specs.append(pl.BlockSpec((wd, tm), lambda i: (0, i)))
            continue
        dts = {"bf16": [BF16], "f32": [F32], "sigmoid": [BF16], "silu": [BF16],
               "hgrn": [F32, BF16]}[kind]
        for dt in dts:
            out_shape.append(jax.ShapeDtypeStruct((N, wd), dt))
            out_specs.append(pl.BlockSpec((tm, wd), lambda i: (i, 0)))
    return pl.pallas_call(
        functools.partial(_proj_kernel, segs=tuple(segs)),
        out_shape=out_shape,
        grid=(N // tm,),
        in_specs=[pl.BlockSpec((tm, D), lambda i: (i, 0)),
                  _resident(w.shape),
                  _resident(b.shape),
                  _resident(lb.shape)],
        out_specs=out_specs,
        compiler_params=_params(("parallel",)),
        name=name,
    )(x2d, w, b, lb)


def _mlstm_kernel(q_ref, kt_ref, v_ref, g_ref, ktc_ref, vc_ref, gc_ref,
                  out_ref, hacc_ref, c_ref, mprev_ref, rows_ref,
                  s0_ref, wi0_ref, fl0_ref, s1_ref, wi1_ref, fl1_ref):
    L = M_CHUNK
    hps = HEADS_PER_STEP
    T = q_ref.shape[0]
    Tc = vc_ref.shape[0]
    nt, nct = T // L, Tc // L
    body_chunks = min(M_BODY_CHUNKS, nt)
    assert nt % body_chunks == 0 and body_chunks % 2 == 0
    row = lax.broadcasted_iota(jnp.int32, (L, L), 0)
    col = lax.broadcasted_iota(jnp.int32, (L, L), 1)
    lane = lax.broadcasted_iota(jnp.int32, (L, LANES), 1)
    krow = lax.broadcasted_iota(jnp.int32, (LANES, 1), 0)
    ones_ext = jnp.ones((L, LANES), BF16)
    R_R, R_LF, R_W, R_M, R_DEC = (k * hps for k in range(5))

    nc = nct + nt
    tris = [(col <= row), (col >= row)]

    def gate_body(n, carry, gref, slot0):
        GT = gref[pl.ds(pl.multiple_of(n * L, L), L), :].T[:4 * hps, :]
        for d in (0, 1):
            lf = jax.nn.log_sigmoid(GT[(2 + d) * hps:(3 + d) * hps, :])
            trif_t = tris[1 - d].astype(F32)
            b = sum(_dot(part.astype(F32), trif_t) for part in _split3(lf))
            r = GT[d * hps:(d + 1) * hps, :] - b
            b_last = b[:, L - 1:L] if d == 0 else b[:, 0:1]
            wlog = b_last + r
            wide = lambda x: jnp.broadcast_to(x, (hps, L))
            rows_ref[d * nc + slot0 + n] = jnp.concatenate(
                [r, lf, wlog, wide(b_last), wide(jnp.max(wlog, axis=1, keepdims=True)),
                 jnp.zeros((rows_ref.shape[1] - 5 * hps, L), F32)], axis=0)
        return carry

    lax.fori_loop(0, nct, functools.partial(gate_body, gref=gc_ref, slot0=0), 0, unroll=True)
    lax.fori_loop(0, nt, functools.partial(gate_body, gref=g_ref, slot0=nct), 0, unroll=8)

    for d in (0, 1):
        tri = tris[d]
        base = d * nc
        c_ref[...] = jnp.zeros(c_ref.shape, F32)

        def stab_body(n, m_prev, nchunks, slot0, d=d, base=base):
            idx = slot0 + (n if d == 0 else nchunks - 1 - n)
            mprev_ref[idx] = m_prev
            return jnp.maximum(rows_ref[base + idx, R_M:R_M + hps, :] + m_prev,
                               rows_ref[base + idx, R_DEC:R_DEC + hps, :])

        def weight_body(n, carry, base=base):
            R = rows_ref[base + n]
            m_prev = mprev_ref[n]
            bm = R[R_M:R_M + hps, :] + m_prev
            m_new = jnp.maximum(bm, R[R_DEC:R_DEC + hps, :])
            rows_ref[base + n, R_W:R_W + 3 * hps, :] = jnp.concatenate(
                [jnp.exp(R[R_W:R_W + hps, :] - m_new), m_prev, jnp.exp(bm - m_new)], axis=0)
            return carry

        m_run = lax.fori_loop(0, nct, functools.partial(stab_body, nchunks=nct, slot0=0),
                              jnp.zeros((hps, L), F32), unroll=True)
        lax.fori_loop(0, nt, functools.partial(stab_body, nchunks=nt, slot0=nct), m_run, unroll=True)
        lax.fori_loop(0, nct + nt, weight_body, 0, unroll=6)

        def update_state(p, R, rows, ktref, vref):
            kT = ktref[p * LANES:(p + 1) * LANES, rows].astype(F32)
            upd = None
            decay_rows = None
            for a in range(2):
                i = 2 * p + a
                vext = jnp.concatenate([vref[rows, i * LANES:(i + 1) * LANES], ones_ext], axis=1)
                rowmask = (krow < M_QK_DIM) if a == 0 else (krow >= M_QK_DIM)
                kw = jnp.where(rowmask, kT * R[R_W + i:R_W + i + 1, :], 0.0)
                u = _dot(kw.astype(BF16).astype(F32), vext.astype(F32))
                upd = u if upd is None else upd + u
                dr = jnp.where(rowmask, R[R_DEC + i:R_DEC + i + 1, 0:1], 0.0)
                decay_rows = dr if decay_rows is None else decay_rows + dr
            c_ref[p] = decay_rows * c_ref[p] + upd

        def ctx_body(n, carry, d=d, base=base, update_state=update_state):
            c = n if d == 0 else nct - 1 - n
            rows = pl.ds(pl.multiple_of(c * L, L), L)
            for p in range(hps // 2):
                update_state(p, rows_ref[base + c], rows, ktc_ref, vc_ref)
            return carry

        lax.fori_loop(0, nct, ctx_body, 0, unroll=True)

        def head_q(rows, p, a):
            q2 = q_ref[rows, p * LANES:(p + 1) * LANES]
            headmask = (lane < M_QK_DIM) if a == 0 else (lane >= M_QK_DIM)
            return jnp.where(headmask, q2, jnp.zeros_like(q2))

        def pair_weights(c, s_ref, wi_ref, fl_ref, tri=tri, base=base, head_q=head_q):
            rows = pl.ds(pl.multiple_of(c * L, L), L)
            R = rows_ref[base + nct + c]
            for i in range(hps):
                p, a = divmod(i, 2)
                s_raw = _dot(head_q(rows, p, a), kt_ref[p * LANES:(p + 1) * LANES, rows])
                m_prev = R[R_M + i:R_M + i + 1, 0:1]
                rm = jnp.where(tri, R[R_R + i:R_R + i + 1, :], -jnp.inf)
                mu = jnp.maximum(m_prev, jnp.max(rm, axis=1, keepdims=True))
                bcol = jnp.sum(jnp.where(tri, R[R_LF + i:R_LF + i + 1, :], 0.0), axis=1, keepdims=True)
                mu_b = jnp.broadcast_to(mu, (L, L))
                s_ref[i] = s_raw * jnp.exp(rm - mu_b)
                wi_ref[i] = jnp.exp(m_prev - mu_b)
                fl_ref[i] = jnp.exp(-jnp.broadcast_to(bcol + mu, (L, LANES)))

        def emit(c, s_ref, wi_ref, fl_ref, d=d, base=base, head_q=head_q, update_state=update_state):
            rows = pl.ds(pl.multiple_of(c * L, L), L)
            for p in range(hps // 2):
                C2b = c_ref[p].astype(BF16)
                for a in range(2):
                    i = 2 * p + a
                    cs = slice(i * LANES, (i + 1) * LANES)
                    vext = jnp.concatenate([v_ref[rows, cs], ones_ext], axis=1)
                    nd = (jnp.concatenate([wi_ref[i]] * 2, axis=1) * _dot(head_q(rows, p, a), C2b)
                          + _dot(s_ref[i].astype(BF16), vext))
                    h = nd[:, :LANES] / jnp.maximum(jnp.abs(nd[:, LANES:]), fl_ref[i])
                    if d == 0:
                        hacc_ref[rows, cs] = h
                    else:
                        out_ref[rows, cs] = (hacc_ref[rows, cs] + h).astype(BF16)
                update_state(p, rows_ref[base + nct + c], rows, kt_ref, v_ref)

        order = (lambda n: n) if d == 0 else (lambda n: nt - 1 - n)
        slot0 = (s0_ref, wi0_ref, fl0_ref)
        slot1 = (s1_ref, wi1_ref, fl1_ref)

        def lat_body(n, carry, pair_weights=pair_weights, emit=emit, order=order):
            slots = (slot0, slot1)
            for k in range(body_chunks):
                c = body_chunks * n + k
                pair_weights(order(jnp.minimum(c + 1, nt - 1)), *slots[(k + 1) % 2])
                emit(order(c), *slots[k % 2])
            return carry

        pair_weights(order(0), *slot0)
        lax.fori_loop(0, nt // body_chunks, lat_body, 0)


def _mlstm(q, kt, v, g, ktc, vc, gc):
    B, T, _ = q.shape
    Tc = vc.shape[1]
    hps = HEADS_PER_STEP
    G = M_HEADS // hps
    wq, wv = hps * M_QK_DIM, hps * M_V_DIM
    lat = lambda w: pl.BlockSpec((None, T, w), lambda b, g: (b, 0, g))
    cx = lambda w: pl.BlockSpec((None, Tc, w), lambda b, g: (b, 0, g))
    return pl.pallas_call(
        _mlstm_kernel,
        out_shape=jax.ShapeDtypeStruct((B, T, M_HEADS * M_V_DIM), BF16),
        grid=(B, G),
        in_specs=[lat(wq), pl.BlockSpec((wq, T), lambda b, g: (g, b)), lat(wv), lat(LANES),
                  pl.BlockSpec((wq, Tc), lambda b, g: (g, b)), cx(wv), cx(LANES)],
        out_specs=lat(wv),
        scratch_shapes=[pltpu.VMEM((T, wv), F32),
                        pltpu.VMEM((hps // 2, 2 * M_QK_DIM, 2 * LANES), F32),
                        pltpu.VMEM(((T + Tc) // M_CHUNK, hps, M_CHUNK), F32),
                        pltpu.VMEM((2 * (T + Tc) // M_CHUNK, 8 * hps, M_CHUNK), F32)]
                       + [pltpu.VMEM((hps, M_CHUNK, M_CHUNK), F32)] * 6,
        compiler_params=_params(("parallel", "parallel")),
        name="mlstm_scan",
    )(q, kt, v, g, ktc, vc, gc)


def _pivot_rows(b, h, off):
    L, N = b.shape
    bc = lambda r, n: jnp.broadcast_to(b[r:r + 1, :], (n, N))
    if 2 * h >= 8:
        return jnp.concatenate([bc(x + off, 2 * h) for x in range(0, L, 2 * h)], axis=0)
    sub = lax.broadcasted_iota(jnp.int32, (8, N), 0)
    tiles = []
    for x in range(0, L, 8):
        t = bc(x + 8 - 2 * h + off, 8)
        for y in range(8 - 4 * h, -1, -2 * h):
            t = jnp.where(sub < y + 2 * h, bc(x + y + off, 8), t)
        tiles.append(t)
    return jnp.concatenate(tiles, axis=0)


def _block_entry_rows(b, beta, d):
    L, N = b.shape
    tiles = []
    for x in range(0, L, beta):
        r = x - 1 if d == 0 else x + beta
        inside = 0 <= r < L
        tiles.append(jnp.broadcast_to(b[r:r + 1, :], (beta, N)) if inside else jnp.zeros((beta, N), F32))
    return jnp.concatenate(tiles, axis=0)


def _hgrn_kernel(q_ref, kf_ref, kb_ref, v_ref, lff_ref, lfb_ref,
                 kfc_ref, kbc_ref, vc_ref, lffc_ref, lfbc_ref,
                 out_ref, hacc_ref, st_ref, b_ref, bc_ref, attn0_ref, attn1_ref, *, beta):
    L = H_CHUNK
    hps = HEADS_PER_STEP
    T = q_ref.shape[0]
    Tc = vc_ref.shape[0]
    nt, nct = T // L, Tc // L
    body_chunks = min(H_BODY_CHUNKS, nt)
    assert nt % body_chunks == 0 and body_chunks % 2 == 0
    row = lax.broadcasted_iota(jnp.int32, (L, L), 0)
    col = lax.broadcasted_iota(jnp.int32, (L, L), 1)
    halves = [h for h in (L >> (n + 1) for n in range(L.bit_length() - 1)) if h >= beta]
    zpad_f = jnp.zeros((H_DIM - L, H_DIM), F32)
    zpad_b = jnp.zeros((H_DIM - L, H_DIM), BF16)
    neg_log2e = -1.4426950408889634

    for d in (0, 1):
        tri = (col <= row) if d == 0 else (col >= row)
        trib = tri.astype(BF16)
        last = L - 1 if d == 0 else 0
        st_ref[...] = jnp.zeros(st_ref.shape, F32)
        k_lat, lf_lat = (kf_ref, lff_ref) if d == 0 else (kb_ref, lfb_ref)
        k_ctx, lf_ctx = (kfc_ref, lffc_ref) if d == 0 else (kbc_ref, lfbc_ref)
        local = tri & ((row & -beta) == (col & -beta))
        pair = []
        for h in halves:
            same = (row & -(2 * h)) == (col & -(2 * h))
            r_late = (row & (2 * h - 1)) >= h
            c_late = (col & (2 * h - 1)) >= h
            pair.append(same & (r_late & ~c_late if d == 0 else ~r_late & c_late))

        def cumsum_body(n, carry, lfref, bref, trib=trib):
            rows = pl.ds(pl.multiple_of(n * L, L), L)
            bref[rows, :] = _tri_cumsum(trib, lfref[rows, :])
            return carry

        lax.fori_loop(0, nct, functools.partial(cumsum_body, lfref=lf_ctx, bref=bc_ref), 0, unroll=True)
        lax.fori_loop(0, nt, functools.partial(cumsum_body, lfref=lf_lat, bref=b_ref), 0, unroll=8)

        def update_state(i, rows, cs, kref, vref, bref):
            b_h = bref[rows, cs]
            bl = b_h[last:last + 1, :]
            kd = (kref[rows, cs].astype(F32) * jnp.exp(bl - b_h)).astype(BF16)
            vT = jnp.concatenate([vref[rows, cs].astype(F32), zpad_f], axis=0).T.astype(BF16)
            st_ref[i] = (st_ref[i] * jnp.exp(bl)
                         + _dot(vT, jnp.concatenate([kd, zpad_b], axis=0)))

        def ctx_body(n, carry, d=d, k_ctx=k_ctx, update_state=update_state):
            c = n if d == 0 else nct - 1 - n
            rows = pl.ds(pl.multiple_of(c * L, L), L)
            for i in range(hps):
                update_state(i, rows, slice(i * H_DIM, (i + 1) * H_DIM), k_ctx, vc_ref, bc_ref)
            return carry

        lax.fori_loop(0, nct, ctx_body, 0, unroll=True)

        def pair_weights(c, attn_ref, d=d, k_lat=k_lat, local=local, pair=pair):
            rows = pl.ds(pl.multiple_of(c * L, L), L)
            for i in range(hps):
                cs = slice(i * H_DIM, (i + 1) * H_DIM)
                b_h = b_ref[rows, cs]
                q_h = q_ref[rows, cs].astype(F32)
                k_h = k_lat[rows, cs].astype(F32)
                if beta > 1:
                    dl = (b_h - _block_entry_rows(b_h, beta, d)) * neg_log2e
                    qs = (q_h * jnp.exp2(-dl)).astype(BF16)
                    ks = (k_h * jnp.exp2(dl)).astype(BF16)
                else:
                    qs, ks = q_ref[rows, cs], k_lat[rows, cs]
                attn = jnp.where(local, lax.dot_general(qs, ks, _NT, preferred_element_type=F32), 0.0)
                for n, h in enumerate(halves):
                    z = jnp.exp2(jnp.abs(b_h - _pivot_rows(b_h, h, h - 1 + d)) * neg_log2e)
                    if 2 * h == L:
                        qr, kr = (slice(h, L), slice(0, h)) if d == 0 else (slice(0, h), slice(h, L))
                        qz = (q_h[qr] * z[qr]).astype(BF16)
                        kz = (k_h[kr] * z[kr]).astype(BF16)
                        zh = jnp.zeros((h, H_DIM), BF16)
                        qs = jnp.concatenate([zh, qz] if d == 0 else [qz, zh], axis=0)
                        ks = jnp.concatenate([kz, zh] if d == 0 else [zh, kz], axis=0)
                        attn = attn + lax.dot_general(qs, ks, _NT, preferred_element_type=F32)
                        continue
                    qk = lax.dot_general((q_h * z).astype(BF16), (k_h * z).astype(BF16), _NT,
                                         preferred_element_type=F32)
                    attn = jnp.where(pair[n], qk, attn)
                attn_ref[i] = attn

        def emit(c, attn_ref, d=d, k_lat=k_lat, update_state=update_state):
            rows = pl.ds(pl.multiple_of(c * L, L), L)
            for i in range(hps):
                cs = slice(i * H_DIM, (i + 1) * H_DIM)
                qc = (q_ref[rows, cs].astype(F32) * jnp.exp(b_ref[rows, cs])).astype(BF16)
                o = (lax.dot_general(qc, st_ref[i].astype(BF16), _NT, preferred_element_type=F32)
                     + _dot(attn_ref[i].astype(BF16), v_ref[rows, cs]))
                if d == 0:
                    hacc_ref[rows, cs] = o
                else:
                    out_ref[rows, cs] = (hacc_ref[rows, cs] + o).astype(BF16)
                update_state(i, rows, cs, k_lat, v_ref, b_ref)

        order = (lambda n: n) if d == 0 else (lambda n: nt - 1 - n)

        def lat_body(n, carry, pair_weights=pair_weights, emit=emit, order=order):
            slots = (attn0_ref, attn1_ref)
            for k in range(body_chunks):
                c = body_chunks * n + k
                pair_weights(order(jnp.minimum(c + 1, nt - 1)), slots[(k + 1) % 2])
                emit(order(c), slots[k % 2])
            return carry

        pair_weights(order(0), attn0_ref)
        lax.fori_loop(0, nt // body_chunks, lat_body, 0)


def _hgrn(beta, q, kf, kb, v, lff, lfb, kfc, kbc, vc, lffc, lfbc):
    B, T, W = q.shape
    Tc = vc.shape[1]
    hps = HEADS_PER_STEP
    G = H_HEADS // hps
    w = hps * H_DIM
    lat = pl.BlockSpec((None, T, w), lambda b, g: (b, 0, g))
    cx = pl.BlockSpec((None, Tc, w), lambda b, g: (b, 0, g))
    return pl.pallas_call(
        functools.partial(_hgrn_kernel, beta=beta),
        out_shape=jax.ShapeDtypeStruct((B, T, W), BF16),
        grid=(B, G),
        in_specs=[lat] * 6 + [cx] * 5,
        out_specs=lat,
        scratch_shapes=[pltpu.VMEM((T, w), F32),
                        pltpu.VMEM((hps, H_DIM, H_DIM), F32),
                        pltpu.VMEM((T, w), F32),
                        pltpu.VMEM((Tc, w), F32),
                        pltpu.VMEM((hps, H_CHUNK, H_CHUNK), F32),
                        pltpu.VMEM((hps, H_CHUNK, H_CHUNK), F32)],
        compiler_params=_params(("parallel", "parallel")),
        name="hgrn_scan_b%d" % beta,
    )(q, kf, kb, v, lff, lfb, kfc, kbc, vc, lffc, lfbc)


def _to_col_major(a):
    B, T, C = a.shape
    return a.reshape(B, T // GRID_W, GRID_W, C).transpose(0, 2, 1, 3).reshape(B, T, C)


def _to_row_major(a):
    B, T, C = a.shape
    return a.reshape(B, GRID_W, T // GRID_W, C).transpose(0, 2, 1, 3).reshape(B, T, C)


def kernel(x, c, ctx, c_ctx, ada_w, ada_b, ffn1_norm, ffn1_w_in, ffn1_w_out, mix_norm,
           mix_w_in, mix_b_in, mlstm_norm, hgrn_lb_logits, hgrn_norm, proj_m, proj_h,
           mix_w_out, ffn2_norm, ffn2_w_in, ffn2_w_out, final_norm):
    B, T, D = x.shape
    Tc = ctx.shape[1]
    hps = HEADS_PER_STEP
    MQ, MV, HW = M_HEADS * M_QK_DIM, M_HEADS * M_V_DIM, H_HEADS * H_DIM

    pad = (-(B + 1)) % 8
    cc = jnp.concatenate([c, c_ctx[None, :], jnp.zeros((pad, D), F32)], axis=0)
    mod = _modulation(cc, ada_w[0], ada_b[0][None, :])
    ml = mod[:B].reshape(B, N_MOD, D)
    mc = mod[B:B + 1].reshape(1, N_MOD, D)

    row = lambda v: v.reshape(1, -1).astype(F32)
    w1_in = ffn1_w_in[0].astype(BF16)
    w1_out = ffn1_w_out[0].astype(BF16)
    w2_in = ffn2_w_in[0].astype(BF16)
    w2_out = ffn2_w_out[0].astype(BF16)

    W, bias = mix_w_in[0], mix_b_in[0]
    o = 0
    src = {}
    for name, wd in (("mq", MQ), ("mk", MQ), ("mv", MV), ("mo", MV), ("ig", 2 * M_HEADS),
                     ("fg", 2 * M_HEADS), ("hq", HW), ("hff", HW), ("hfb", HW), ("hi", HW),
                     ("hg", HW), ("gm", D), ("gh", D)):
        src[name] = (o, o + wd)
        o += wd
    ws, bs, col, o = [], [], {}, 0
    for name in ("mq", "mk", "mv", "mo", "hq", "hff", "hfb", "hi", "hg", "gm", "gh"):
        a, b = src[name]
        scale = M_QK_DIM ** -0.5 if name == "mq" else 1.0
        ws.append(W[:, a:b] * scale)
        bs.append(bias[a:b] * scale)
        col[name] = o
        o += b - a
    col["gates"] = o
    for g in range(M_HEADS // hps):
        for n in ("ig", "fg"):
            for off in (0, M_HEADS):
                a = src[n][0] + off + g * hps
                ws.append(W[:, a:a + hps])
                bs.append(bias[a:a + hps])
        ws.append(jnp.zeros((D, LANES - 4 * hps), F32))
        bs.append(jnp.zeros((LANES - 4 * hps,), F32))
    w_mix = jnp.concatenate(ws, axis=1).astype(BF16)
    b_mix = jnp.concatenate(bs)[None, :]

    n_gate = LANES * (M_HEADS // hps)
    lb = jnp.cumsum(jax.nn.softmax(hgrn_lb_logits.astype(F32), axis=1), axis=1)[:, 0]

    x2d = x.reshape(B * T, D)
    c2d = ctx.reshape(B * Tc, D)
    tpb = T // TOKEN_TILE
    (xm_c,) = _ffn(c2d, mc, None, row(ffn1_norm[0]), w1_in, w1_out, row(mix_norm[0]),
                   rows=(0, 1, 2), rows2=(3, 4), out_x=False, aux="mix")
    x1, xm_l = _ffn(x2d, ml, tpb, row(ffn1_norm[0]), w1_in, w1_out, row(mix_norm[0]),
                    rows=(0, 1, 2), rows2=(3, 4), out_x=True, aux="mix")

    seg = lambda name, wd, kind, arg=0: (col[name], wd, kind, arg)
    mq, mk, mv, mso, mg, sgm, sgh, hsg = _proj(
        xm_l, w_mix, b_mix, lb,
        [seg("mq", MQ, "bf16"), seg("mk", MQ, "bf16_t"), seg("mv", MV, "bf16"),
         seg("mo", MV, "sigmoid"), seg("gates", n_gate, "f32"), seg("gm", D, "sigmoid"),
         seg("gh", D, "sigmoid"), seg("hg", HW, "silu")], "proj_mlstm")
    mkc, mvc, mgc, lffc, hkfc, lfbc, hkbc, hvc = _proj(
        xm_c, w_mix, b_mix, lb,
        [seg("mk", MQ, "bf16_t"), seg("mv", MV, "bf16"), seg("gates", n_gate, "f32"),
         seg("hff", HW, "hgrn", 0), seg("hfb", HW, "hgrn", 1), seg("hi", HW, "bf16")],
        "proj_ctx")

    xm_cm = _to_col_major(xm_l.reshape(B, T, D)).reshape(B * T, D)
    hq, lff, hkf, lfb, hkb, hv = _proj(
        xm_cm, w_mix, b_mix, lb,
        [seg("hq", HW, "silu"), seg("hff", HW, "hgrn", 0), seg("hfb", HW, "hgrn", 1),
         seg("hi", HW, "bf16")], "proj_hgrn")

    r3 = lambda a, t: a.reshape(B, t, a.shape[-1])
    hm = _mlstm(r3(mq, T), mk, r3(mv, T), r3(mg, T), mkc, r3(mvc, Tc), r3(mgc, Tc))
    hgrn_args = (r3(hq, T), r3(hkf, T), r3(hkb, T), r3(hv, T), r3(lff, T), r3(lfb, T),
                 r3(hkfc, Tc), r3(hkbc, Tc), r3(hvc, Tc), r3(lffc, Tc), r3(lfbc, Tc))
    block_ok = H_BLOCK * jnp.max(-jnp.log(lb)) <= H_BLOCK_MAX_EXPONENT
    hh_cm = lax.cond(block_ok, functools.partial(_hgrn, H_BLOCK), functools.partial(_hgrn, 1),
                     *hgrn_args)
    hh = _to_row_major(hh_cm)

    merge = (hm.reshape(B * T, MV), hh.reshape(B * T, HW), mso, hsg, sgm, sgh,
             row(mlstm_norm[0]), row(hgrn_norm[0]),
             proj_m[0].astype(BF16), proj_h[0].astype(BF16), mix_w_out[0].astype(BF16), 5)
    (out,) = _ffn(x1, ml, tpb, row(ffn2_norm[0]), w2_in, w2_out, row(final_norm),
                  rows=(6, 7, 8), rows2=(3, 4), out_x=False, aux="final", merge=merge)
    return out.reshape(B, T, D)
```
